```python
import math
import jax, jax.numpy as jnp
from jax import lax
import numpy as np

D_MODEL = 1024
BATCH = 16
SEQ = 256
DEPTH = 4
DEC_BATCH = 2
DEC_SEQ = 2048
PAST_LEN = 256

GRID_W = 64
N_ATTN_LAYERS = (DEPTH + 1) // 2
N_POOL_LAYERS = DEPTH // 2
A_HEADS = 4
A_HEAD_DIM = 64
B_Q_HEADS = 8
B_KV_HEADS = 2
B_HEAD_DIM = 64
Q_BLOCK = 128
ROPE_THETA = 10000.0
A_QKV_W = A_HEADS * 2 * A_HEAD_DIM
B_Q_W = B_Q_HEADS * B_HEAD_DIM
B_KV_W = B_KV_HEADS * B_HEAD_DIM
IN_PROJ_W = 3 * A_QKV_W + B_Q_W + 2 * B_KV_W
MIX_OUT_W = A_QKV_W + B_Q_W
POOL_WINDOWS = (2, 4, 8, 16)
POOL_GROUPS = 4
POOL_CH = D_MODEL // POOL_GROUPS
N_EXPERTS = 32
TOP_K = 4
D_EXPERT = D_MODEL
SWIGLU_LIMIT = 7.0
SWIGLU_ALPHA = 1.702
N_MOD = 6
EPS = 1e-6

kernel_name = "hybrid_diffusion_prefix_trunk_step"


def _rmsnorm(x, g):
    xf = x.astype(jnp.float32)
    xf = xf * lax.rsqrt(jnp.mean(xf * xf, axis=-1, keepdims=True) + EPS)
    return (xf * g.astype(jnp.float32)).astype(x.dtype)


def _modulation(cond, w, b):
    m = jax.nn.silu(cond) @ w + b
    return jnp.split(m[:, None, :], N_MOD, axis=-1)


def _modulate(h, shift, scale):
    return h * (1.0 + scale) + shift


def _rope_tables(n_tokens, dim):
    n_rows = n_tokens // GRID_W
    rows = jnp.repeat(jnp.arange(n_rows, dtype=jnp.float32), GRID_W)
    cols = jnp.tile(jnp.arange(GRID_W, dtype=jnp.float32), n_rows)
    axis_dim = dim // 2
    inv = ROPE_THETA ** (-jnp.arange(0, axis_dim, 2, dtype=jnp.float32) / axis_dim)
    ang_r = rows[:, None] * inv[None, :]
    ang_c = cols[:, None] * inv[None, :]
    ang = jnp.concatenate([ang_r, ang_r, ang_c, ang_c], axis=-1)
    return jnp.cos(ang), jnp.sin(ang)


def _rotate_half_axial(x):
    r1, r2, c1, c2 = jnp.split(x, 4, axis=-1)
    return jnp.concatenate([-r2, r1, -c2, c1], axis=-1)


def _apply_rope(x, cos, sin):
    shape = (1, x.shape[1]) + (1,) * (x.ndim - 3) + (x.shape[-1],)
    c = cos.reshape(shape).astype(x.dtype)
    s = sin.reshape(shape).astype(x.dtype)
    return x * c + _rotate_half_axial(x) * s


def _rope_diff(x, cos, sin):
    B, L, H, _ = x.shape
    return _apply_rope(x.reshape(B, L, H, 2, A_HEAD_DIM), cos, sin).reshape(B, L, H, 2 * A_HEAD_DIM)


def _attend(q, k, v, scale):
    B, Sq, H, Dq = q.shape
    G = k.shape[2]
    rep = H // G
    Dv = v.shape[-1]
    nb = Sq // Q_BLOCK
    qb = q.reshape(B, nb, Q_BLOCK, G, rep, Dq).transpose(1, 0, 2, 3, 4, 5)

    def one_block(q_blk):
        s = jnp.einsum("bqgrd,bkgd->bgrqk", q_blk, k).astype(jnp.float32) * scale
        p = jax.nn.softmax(s, axis=-1).astype(v.dtype)
        return jnp.einsum("bgrqk,bkgv->bqgrv", p, v)

    out = lax.map(one_block, qb)
    return out.transpose(1, 0, 2, 3, 4, 5).reshape(B, Sq, H, Dv)


def _attn_qkv(h, w_in, qn_g, kn_g):
    B, L, _ = h.shape
    p = h @ w_in
    i1 = A_QKV_W
    i2 = 2 * A_QKV_W
    i3 = 3 * A_QKV_W
    i4 = i3 + B_Q_W
    i5 = i4 + B_KV_W
    a_q = p[..., :i1].reshape(B, L, A_HEADS, 2 * A_HEAD_DIM)
    a_k = p[..., i1:i2].reshape(B, L, A_HEADS, 2 * A_HEAD_DIM)
    a_v = p[..., i2:i3].reshape(B, L, A_HEADS, 2 * A_HEAD_DIM)
    b_q = _rmsnorm(p[..., i3:i4].reshape(B, L, B_Q_HEADS, B_HEAD_DIM), qn_g)
    b_k = _rmsnorm(p[..., i4:i5].reshape(B, L, B_KV_HEADS, B_HEAD_DIM), kn_g)
    b_v = p[..., i5:].reshape(B, L, B_KV_HEADS, B_HEAD_DIM)
    return a_q, a_k, a_v, b_q, b_k, b_v


def _attn_out(a_q, a_k, a_v, b_q, b_k, b_v, w_out, lam_q1, lam_k1, lam_q2, lam_k2,
              subln_g, lambda_init):
    B, L = a_q.shape[:2]
    f32 = jnp.float32
    lam = (jnp.exp(jnp.sum(lam_q1.astype(f32) * lam_k1.astype(f32)))
           - jnp.exp(jnp.sum(lam_q2.astype(f32) * lam_k2.astype(f32)))
           + lambda_init).astype(a_v.dtype)
    sa = A_HEAD_DIM ** -0.5
    o1 = _attend(a_q[..., :A_HEAD_DIM], a_k[..., :A_HEAD_DIM], a_v, sa)
    o2 = _attend(a_q[..., A_HEAD_DIM:], a_k[..., A_HEAD_DIM:], a_v, sa)
    o_a = _rmsnorm(o1 - lam * o2, subln_g) * (1.0 - lambda_init)
    o_b = _attend(b_q, b_k, b_v, B_HEAD_DIM ** -0.5)
    o = jnp.concatenate([o_a.reshape(B, L, -1), o_b.reshape(B, L, -1)], axis=-1)
    return o @ w_out


def _pool_mixer(h, w_pool, pool_scale):
    B, L, D = h.shape
    hf = h.astype(jnp.float32)
    cs = jnp.cumsum(hf, axis=1)
    cs = jnp.concatenate([jnp.zeros((B, 1, D), jnp.float32), cs], axis=1)
    cs = cs.reshape(B, L + 1, POOL_GROUPS, POOL_CH)
    t = jnp.arange(L)[:, None]
    half = jnp.array(POOL_WINDOWS, dtype=jnp.int32)[None, :] // 2
    lo = jnp.clip(t - half, 0, L)
    hi = jnp.clip(t + half, 0, L)
    g_idx = jnp.arange(POOL_GROUPS)[None, :]
    count = (hi - lo).astype(jnp.float32)
    mean = (cs[:, hi, g_idx] - cs[:, lo, g_idx]) / count[None, :, :, None]
    pooled = (mean - hf.reshape(B, L, POOL_GROUPS, POOL_CH)).astype(h.dtype)
    y = jnp.einsum("blgc,gce->blge", pooled, w_pool).reshape(B, L, D)
    return y * pool_scale


def _moe(h, w_router, b_router, w_gu, b_gu, w_down, b_down):
    B, L, D = h.shape
    x = h.reshape(B * L, D)
    logits = (x @ w_router + b_router).astype(jnp.float32)
    top_v, top_i = lax.top_k(logits, TOP_K)
    gates = jax.nn.softmax(top_v, axis=-1)
    onehot = jax.nn.one_hot(top_i, N_EXPERTS, dtype=jnp.float32)
    gate_et = jnp.einsum("tk,tke->et", gates, onehot)

    def expert(acc, params):
        wgu, bgu, wd, bd, g = params
        gu = x @ wgu + bgu
        gate, up = jnp.split(gu, 2, axis=-1)
        gate = jnp.minimum(gate, SWIGLU_LIMIT)
        up = jnp.clip(up, -SWIGLU_LIMIT, SWIGLU_LIMIT)
        act = gate * jax.nn.sigmoid(SWIGLU_ALPHA * gate) * (up + 1.0)
        y = act @ wd + bd
        return acc + g[:, None] * y.astype(jnp.float32), None

    out, _ = lax.scan(expert, jnp.zeros((B * L, D), jnp.float32),
                      (w_gu, b_gu, w_down, b_down, gate_et))
    return out.astype(h.dtype).reshape(B, L, D)


def setup_inputs(seed: int = 0) -> dict:
    key = jax.random.key(seed)
    ks = jax.random.split(key, 32)
    f32 = jnp.float32
    nrm = lambda k, shape, s: (jax.random.normal(k, shape, f32) * s)
    gain = lambda k, shape: 1.0 + 0.05 * jax.random.normal(k, shape, f32)
    return {
        "x_prompt": nrm(ks[0], (BATCH, SEQ, D_MODEL), 1.0),
        "x_sample": nrm(ks[1], (DEC_BATCH, DEC_SEQ, D_MODEL), 1.0),
        "cache_diff_k": nrm(ks[2], (DEC_BATCH, N_ATTN_LAYERS, PAST_LEN, A_HEADS, 2 * A_HEAD_DIM), 1.0),
        "cache_diff_v": nrm(ks[3], (DEC_BATCH, N_ATTN_LAYERS, PAST_LEN, A_HEADS, 2 * A_HEAD_DIM), 1.0),
        "cache_gqa_k": nrm(ks[4], (DEC_BATCH, N_ATTN_LAYERS, PAST_LEN, B_KV_HEADS, B_HEAD_DIM), 1.0),
        "cache_gqa_v": nrm(ks[5], (DEC_BATCH, N_ATTN_LAYERS, PAST_LEN, B_KV_HEADS, B_HEAD_DIM), 1.0),
        "c": nrm(ks[6], (DEC_BATCH, D_MODEL), 1.0),
        "c_ctx": nrm(ks[7], (D_MODEL,), 1.0),
        "w_ada": nrm(ks[8], (DEPTH, D_MODEL, N_MOD * D_MODEL), 0.5 * D_MODEL ** -0.5),
        "b_ada": nrm(ks[9], (DEPTH, N_MOD * D_MODEL), 0.01),
        "norm_mix": gain(ks[10], (DEPTH, D_MODEL)),
        "norm_ffn": gain(ks[11], (DEPTH, D_MODEL)),
        "norm_final": gain(ks[12], (D_MODEL,)),
        "w_attn_in": nrm(ks[13], (N_ATTN_LAYERS, D_MODEL, IN_PROJ_W), D_MODEL ** -0.5),
        "w_attn_out": nrm(ks[14], (N_ATTN_LAYERS, MIX_OUT_W, D_MODEL), MIX_OUT_W ** -0.5),
        "lam_q1": nrm(ks[15], (N_ATTN_LAYERS, A_HEAD_DIM), 0.1),
        "lam_k1": nrm(ks[16], (N_ATTN_LAYERS, A_HEAD_DIM), 0.1),
        "lam_q2": nrm(ks[17], (N_ATTN_LAYERS, A_HEAD_DIM), 0.1),
        "lam_k2": nrm(ks[18], (N_ATTN_LAYERS, A_HEAD_DIM), 0.1),
        "diff_subln": gain(ks[19], (N_ATTN_LAYERS, 2 * A_HEAD_DIM)),
        "gqa_q_norm": gain(ks[20], (N_ATTN_LAYERS, B_HEAD_DIM)),
        "gqa_k_norm": gain(ks[21], (N_ATTN_LAYERS, B_HEAD_DIM)),
        "w_pool": nrm(ks[22], (N_POOL_LAYERS, POOL_GROUPS, POOL_CH, POOL_CH), POOL_CH ** -0.5),
        "pool_scale": 1.0 + 0.1 * jax.random.normal(ks[23], (N_POOL_LAYERS, D_MODEL), f32),
        "w_router": nrm(ks[24], (DEPTH, D_MODEL, N_EXPERTS), D_MODEL ** -0.5),
        "b_router": nrm(ks[25], (DEPTH, N_EXPERTS), 0.01),
        "w_gate_up": nrm(ks[26], (DEPTH, N_EXPERTS, D_MODEL, 2 * D_EXPERT), D_MODEL ** -0.5),
        "b_gate_up": nrm(ks[27], (DEPTH, N_EXPERTS, 2 * D_EXPERT), 0.01),
        "w_down": nrm(ks[28], (DEPTH, N_EXPERTS, D_EXPERT, D_MODEL), D_EXPERT ** -0.5),
        "b_down": nrm(ks[29], (DEPTH, N_EXPERTS, D_MODEL), 0.01),
    }


def reference(x_prompt, x_sample, cache_diff_k, cache_diff_v, cache_gqa_k, cache_gqa_v,
              c, c_ctx, w_ada, b_ada, norm_mix, norm_ffn, norm_final, w_attn_in, w_attn_out,
              lam_q1, lam_k1, lam_q2, lam_k2, diff_subln, gqa_q_norm, gqa_k_norm,
              w_pool, pool_scale, w_router, b_router, w_gate_up, b_gate_up, w_down, b_down):
    n_lat = x_sample.shape[1]
    cos, sin = _rope_tables(n_lat, A_HEAD_DIM)
    ctx_cond = c_ctx[None, :]
    xp, xs = x_prompt, x_sample
    st_dk, st_dv, st_gk, st_gv = [], [], [], []
    for i in range(DEPTH):
        mp = _modulation(ctx_cond, w_ada[i], b_ada[i])
        ms = _modulation(c, w_ada[i], b_ada[i])
        hp = _modulate(_rmsnorm(xp, norm_mix[i]), mp[0], mp[1])
        hs = _modulate(_rmsnorm(xs, norm_mix[i]), ms[0], ms[1])
        j = i // 2
        if i % 2 == 0:
            lambda_init = 0.8 - 0.6 * math.exp(-0.3 * i)
            a_q, a_k, a_v, b_q, b_k, b_v = _attn_qkv(hp, w_attn_in[j], gqa_q_norm[j], gqa_k_norm[j])
            yp = _attn_out(a_q, a_k, a_v, b_q, b_k, b_v, w_attn_out[j], lam_q1[j], lam_k1[j],
                           lam_q2[j], lam_k2[j], diff_subln[j], lambda_init)
            st_dk.append(a_k)
            st_dv.append(a_v)
            st_gk.append(b_k)
            st_gv.append(b_v)
            a_q, a_k, a_v, b_q, b_k, b_v = _attn_qkv(hs, w_attn_in[j], gqa_q_norm[j], gqa_k_norm[j])
            a_q = _rope_diff(a_q, cos, sin)
            a_k = jnp.concatenate([_rope_diff(a_k, cos, sin), cache_diff_k[:, j]], axis=1)
            a_v = jnp.concatenate([a_v, cache_diff_v[:, j]], axis=1)
            b_q = _apply_rope(b_q, cos, sin)
            b_k = jnp.concatenate([_apply_rope(b_k, cos, sin), cache_gqa_k[:, j]], axis=1)
            b_v = jnp.concatenate([b_v, cache_gqa_v[:, j]], axis=1)
            ys = _attn_out(a_q, a_k, a_v, b_q, b_k, b_v, w_attn_out[j], lam_q1[j], lam_k1[j],
                           lam_q2[j], lam_k2[j], diff_subln[j], lambda_init)
        else:
            yp = _pool_mixer(hp, w_pool[j], pool_scale[j])
            ys = _pool_mixer(hs, w_pool[j], pool_scale[j])
        xp = xp + mp[2] * yp
        xs = xs + ms[2] * ys
        hp = _modulate(_rmsnorm(xp, norm_ffn[i]), mp[3], mp[4])
        hs = _modulate(_rmsnorm(xs, norm_ffn[i]), ms[3], ms[4])
        xp = xp + mp[5] * _moe(hp, w_router[i], b_router[i], w_gate_up[i], b_gate_up[i],
                               w_down[i], b_down[i])
        xs = xs + ms[5] * _moe(hs, w_router[i], b_router[i], w_gate_up[i], b_gate_up[i],
                               w_down[i], b_down[i])
    y_prompt = _rmsnorm(xp, norm_final)
    y_sample = _rmsnorm(xs, norm_final)
    new_diff_k = jnp.stack(st_dk, axis=1)
    new_diff_v = jnp.stack(st_dv, axis=1)
    new_gqa_k = jnp.stack(st_gk, axis=1)
    new_gqa_v = jnp.stack(st_gv, axis=1)
    return (y_prompt, y_sample, new_diff_k, new_diff_v, new_gqa_k, new_gqa_v)
```

```python
import functools
import math

import jax
import jax.numpy as jnp
from jax import lax
from jax.experimental import pallas as pl
from jax.experimental.pallas import tpu as pltpu

F32 = jnp.float32
BF16 = jnp.bfloat16
I32 = jnp.int32

D_MODEL = 1024
DEPTH = 4
GRID_W = 64
A_HEADS = 4
A_HEAD_DIM = 64
B_Q_HEADS = 8
B_KV_HEADS = 2
B_HEAD_DIM = 64
ROPE_THETA = 10000.0
A_W = A_HEADS * 2 * A_HEAD_DIM
BQ_W = B_Q_HEADS * B_HEAD_DIM
BKV_W = B_KV_HEADS * B_HEAD_DIM
IN_W = 3 * A_W + BQ_W + 2 * BKV_W
POOL_WINDOWS = (2, 4, 8, 16)
POOL_GROUPS = 4
POOL_CH = D_MODEL // POOL_GROUPS
N_EXPERTS = 32
TOP_K = 4
D_EXPERT = D_MODEL
SWIGLU_LIMIT = 7.0
SWIGLU_ALPHA = 1.702
N_MOD = 6
EPS = 1e-6

LANES = 128
SUBLANES = 8
TM = 256
HALO = 8
ADA_TN = 1536
FF_CHUNK = 512
VMEM_LIMIT = 56 * 1024 * 1024
NEG_BIG = -1e30


def _cparams(n_axes, vmem=VMEM_LIMIT):
    return pltpu.CompilerParams(
        dimension_semantics=("arbitrary",) * n_axes, vmem_limit_bytes=vmem)


def _dot(a, b):
    return jnp.dot(a, b, preferred_element_type=F32)


def _dot_nt(a, b):
    return lax.dot_general(a, b, (((1,), (1,)), ((), ())), preferred_element_type=F32)


def _rms(x, g):
    ms = jnp.mean(x * x, axis=-1, keepdims=True)
    return x * lax.rsqrt(ms + EPS) * g


def _lane_iota(shape):
    return lax.broadcasted_iota(I32, shape, len(shape) - 1)


def _ada_kernel(c_ref, w_ref, b_ref, o_ref):
    c = c_ref[...]
    s = (c * jax.nn.sigmoid(c)).astype(BF16)
    o_ref[0] = _dot(s, w_ref[0].astype(BF16)) + b_ref[0]


def _modulation_all(cond8, w_ada, b_ada):
    nmod = w_ada.shape[-1]
    return pl.pallas_call(
        _ada_kernel,
        grid=(DEPTH, nmod // ADA_TN),
        in_specs=[
            pl.BlockSpec((SUBLANES, D_MODEL), lambda l, n: (0, 0)),
            pl.BlockSpec((1, D_MODEL, ADA_TN), lambda l, n: (l, 0, n)),
            pl.BlockSpec((1, 1, ADA_TN), lambda l, n: (l, 0, n)),
        ],
        out_specs=pl.BlockSpec((1, SUBLANES, ADA_TN), lambda l, n: (l, 0, n)),
        out_shape=jax.ShapeDtypeStruct((DEPTH, SUBLANES, nmod), F32),
        compiler_params=_cparams(2),
    )(cond8, w_ada, b_ada.reshape(DEPTH, 1, nmod))


def _seg_meansq(x, seg):
    x2 = x * x
    hi = x2.astype(BF16)
    lo = (x2 - hi.astype(F32)).astype(BF16)
    return (_dot(hi, seg) + _dot(lo, seg)) * (1.0 / B_HEAD_DIM)


def _rope128(x, cos, sin_signed):
    lo16 = (_lane_iota(x.shape) % 32) < 16
    nxt = pltpu.roll(x, LANES - 16, 1)
    prv = pltpu.roll(x, 16, 1)
    return x * cos + jnp.where(lo16, nxt, prv) * sin_signed


def _rope(x, cos, sin_signed):
    cols = [_rope128(x[:, c:c + LANES], cos, sin_signed) for c in range(0, x.shape[1], LANES)]
    return cols[0] if len(cols) == 1 else jnp.concatenate(cols, axis=1)


def _qkv_kernel(n_prompt_tiles, x_ref, mod_ref, g_ref, w_ref, gq_ref, gk_ref, cos_ref, sin_ref,
                seg_ref, qa_ref, ka_ref, va_ref, qb_ref, kbd_ref, vb_ref,
                ck_ref, cv_ref, cgk_ref, cgv_ref):
    i = pl.program_id(0)
    mod = mod_ref[0]
    shift = mod[:, 0:D_MODEL]
    scale = mod[:, D_MODEL:2 * D_MODEL]
    h = _rms(x_ref[...], g_ref[...]) * (1.0 + scale) + shift
    p = _dot(h.astype(BF16), w_ref[...])
    a_q = p[:, 0:A_W]
    a_k = p[:, A_W:2 * A_W]
    a_v = p[:, 2 * A_W:3 * A_W]
    o = 3 * A_W
    b_q = p[:, o:o + BQ_W]
    b_k = p[:, o + BQ_W:o + BQ_W + BKV_W]
    b_v = p[:, o + BQ_W + BKV_W:o + BQ_W + 2 * BKV_W]
    seg = seg_ref[...]
    b_q = b_q * lax.rsqrt(_seg_meansq(b_q, seg) + EPS) * gq_ref[...]
    b_k = b_k * lax.rsqrt(_seg_meansq(b_k, seg[:BKV_W, :BKV_W]) + EPS) * gk_ref[...]

    @pl.when(i < n_prompt_tiles)
    def _():
        ck_ref[...] = a_k
        cv_ref[...] = a_v
        cgk_ref[...] = b_k
        cgv_ref[...] = b_v

    cos = cos_ref[...]
    sin = sin_ref[...]
    sm = A_HEAD_DIM ** -0.5
    qa_ref[...] = (_rope(a_q, cos, sin) * sm).astype(BF16)
    ka_ref[...] = _rope(a_k, cos, sin).astype(BF16)
    va_ref[...] = a_v.astype(BF16)
    qb_ref[...] = (_rope(b_q, cos, sin) * (B_HEAD_DIM ** -0.5)).astype(BF16)
    kb = _rope(b_k, cos, sin)
    kb_sw = pltpu.roll(kb, B_HEAD_DIM, 1)
    lo = _lane_iota(kb.shape) < B_HEAD_DIM
    kbd_ref[...] = jnp.concatenate(
        [jnp.where(lo, kb, kb_sw), jnp.where(lo, kb_sw, kb)], axis=1).astype(BF16)
    vb_ref[...] = b_v.astype(BF16)


def _cond_row(i, npt, tiles_per_seq):
    return jnp.where(i < npt, 0, 1 + (i - npt) // tiles_per_seq)


def _qkv_call(x, mod3, layer, g_mix, w_in_bf, gq, gk, cos_t, sin_t, seg, npt, tps):
    t = x.shape[0]
    nt = t // TM
    tp = npt * TM
    row = lambda w: pl.BlockSpec((TM, w), lambda i: (i, 0))
    full = lambda a: pl.BlockSpec(a.shape, lambda i: (0,) * a.ndim)
    tab = pl.BlockSpec((TM, LANES), lambda i: (jnp.where(i < npt, 0, 1 + (i - npt) % tps), 0))
    cache = lambda w: pl.BlockSpec((TM, w), lambda i: (jnp.minimum(i, npt - 1), 0))
    return pl.pallas_call(
        functools.partial(_qkv_kernel, npt),
        grid=(nt,),
        in_specs=[
            row(D_MODEL),
            pl.BlockSpec((1, 1, N_MOD * D_MODEL), lambda i: (layer * 3 + _cond_row(i, npt, tps), 0, 0)),
            full(g_mix), full(w_in_bf), full(gq), full(gk), tab, tab, full(seg),
        ],
        out_specs=[row(A_W), row(A_W), row(A_W), row(BQ_W), row(2 * BKV_W), row(BKV_W),
                   cache(A_W), cache(A_W), cache(BKV_W), cache(BKV_W)],
        out_shape=[
            jax.ShapeDtypeStruct((t, A_W), BF16), jax.ShapeDtypeStruct((t, A_W), BF16),
            jax.ShapeDtypeStruct((t, A_W), BF16), jax.ShapeDtypeStruct((t, BQ_W), BF16),
            jax.ShapeDtypeStruct((t, 2 * BKV_W), BF16), jax.ShapeDtypeStruct((t, BKV_W), BF16),
            jax.ShapeDtypeStruct((tp, A_W), F32), jax.ShapeDtypeStruct((tp, A_W), F32),
            jax.ShapeDtypeStruct((tp, BKV_W), F32), jax.ShapeDtypeStruct((tp, BKV_W), F32),
        ],
        compiler_params=_cparams(1),
    )(x, mod3, g_mix, w_in_bf, gq, gk, cos_t, sin_t, seg)


def _softmax_pv(qq, ks, vs):
    ss = [_dot_nt(qq, k) for k in ks]
    m = ss[0].max(axis=-1, keepdims=True)
    for s in ss[1:]:
        m = jnp.maximum(m, s.max(axis=-1, keepdims=True))
    acc = None
    l = None
    for s, v in zip(ss, vs):
        p = jnp.exp(s - m)
        ls = p.sum(axis=-1, keepdims=True)
        pv = _dot(p.astype(BF16), v)
        acc = pv if acc is None else acc + pv
        l = ls if l is None else l + ls
    return acc / l


def _split_halves(q):
    lo = _lane_iota(q.shape) < (LANES // 2)
    zero = jnp.zeros_like(q)
    return jnp.concatenate([jnp.where(lo, q, zero), jnp.where(lo, zero, q)], axis=0)


def _attn_body(lambda_init, qa, qb, ka_segs, va_segs, kbd_segs, vb_segs, lam, subln):
    tq = qa.shape[0]
    cols = []
    for h in range(A_HEADS):
        sl = slice(h * LANES, (h + 1) * LANES)
        o = _softmax_pv(_split_halves(qa[:, sl]), [k[:, sl] for k in ka_segs],
                        [v[:, sl] for v in va_segs])
        od = o[:tq] - lam * o[tq:]
        cols.append(_rms(od, subln) * (1.0 - lambda_init))
    lo = _lane_iota((tq, LANES)) < B_HEAD_DIM
    for c in range(BQ_W // LANES):
        g = (2 * c) // (B_Q_HEADS // B_KV_HEADS)
        sl = slice(c * LANES, (c + 1) * LANES)
        gl = slice(g * LANES, (g + 1) * LANES)
        o = _softmax_pv(_split_halves(qb[:, sl]), [k[:, gl] for k in kbd_segs], vb_segs)
        oe, oo = o[:tq], o[tq:]
        if g == 0:
            cols.append(jnp.where(lo, oe, pltpu.roll(oo, B_HEAD_DIM, 1)))
        else:
            cols.append(jnp.where(lo, pltpu.roll(oe, B_HEAD_DIM, 1), oo))
    return jnp.concatenate(cols, axis=1).astype(BF16)


def _lambda(lq1, lk1, lq2, lk2, lambda_init):
    return (jnp.exp(jnp.sum(lq1 * lk1, axis=-1, keepdims=True))
            - jnp.exp(jnp.sum(lq2 * lk2, axis=-1, keepdims=True)) + lambda_init)


def _attn_prompt_kernel(lambda_init, qa, ka, va, qb, kbd, vb, lq1, lk1, lq2, lk2, subln, o_ref):
    lam = _lambda(lq1[...], lk1[...], lq2[...], lk2[...], lambda_init)
    o_ref[...] = _attn_body(lambda_init, qa[...], qb[...], [ka[...]], [va[...]],
                            [kbd[...]], [vb[...]], lam, subln[...])


def _attn_sample_kernel(lambda_init, qa, ka, va, qb, kbd, vb, cka, cva, ckbd, cvb,
                        lq1, lk1, lq2, lk2, subln, o_ref):
    lam = _lambda(lq1[...], lk1[...], lq2[...], lk2[...], lambda_init)
    o_ref[...] = _attn_body(lambda_init, qa[...], qb[...], [ka[...], cka[...]], [va[...], cva[...]],
                            [kbd[...], ckbd[...]], [vb[...], cvb[...]], lam, subln[...])


def _attention(lambda_init, qa, ka, va, qb, kbd, vb, cka, cva, ckbd, cvb,
               lq1, lk1, lq2, lk2, subln, n_prompt, prompt_len, n_sample, sample_len):
    npt = n_prompt * prompt_len // TM
    small = [lq1, lk1, lq2, lk2, subln]
    widths = [A_W, A_W, A_W, BQ_W, 2 * BKV_W, BKV_W]
    assert prompt_len == TM
    full1 = lambda a: pl.BlockSpec(a.shape, lambda b: (0,) * a.ndim)
    o_p = pl.pallas_call(
        functools.partial(_attn_prompt_kernel, lambda_init),
        grid=(n_prompt,),
        in_specs=[pl.BlockSpec((TM, w), lambda b: (b, 0)) for w in widths] + [full1(a) for a in small],
        out_specs=pl.BlockSpec((TM, D_MODEL), lambda b: (b, 0)),
        out_shape=jax.ShapeDtypeStruct((npt * TM, D_MODEL), BF16),
        compiler_params=_cparams(1),
    )(qa, ka, va, qb, kbd, vb, *small)
    tps = sample_len // TM
    seq0 = n_prompt * prompt_len // sample_len
    assert seq0 * sample_len == n_prompt * prompt_len
    past = cka.shape[1]
    qspec = lambda w: pl.BlockSpec((TM, w), lambda b, q: (npt + b * tps + q, 0))
    kspec = lambda w: pl.BlockSpec((sample_len, w), lambda b, q: (seq0 + b, 0))
    cspec = lambda w: pl.BlockSpec((None, past, w), lambda b, q: (b, 0, 0))
    full2 = lambda a: pl.BlockSpec(a.shape, lambda b, q: (0,) * a.ndim)
    o_s = pl.pallas_call(
        functools.partial(_attn_sample_kernel, lambda_init),
        grid=(n_sample, tps),
        in_specs=[qspec(A_W), kspec(A_W), kspec(A_W), qspec(BQ_W), kspec(2 * BKV_W), kspec(BKV_W),
                  cspec(A_W), cspec(A_W), cspec(2 * BKV_W), cspec(BKV_W)] + [full2(a) for a in small],
        out_specs=pl.BlockSpec((TM, D_MODEL), lambda b, q: (b * tps + q, 0)),
        out_shape=jax.ShapeDtypeStruct((n_sample * sample_len, D_MODEL), BF16),
        compiler_params=_cparams(2),
    )(qa, ka, va, qb, kbd, vb, cka, cva, ckbd, cvb, *small)
    return jnp.concatenate([o_p, o_s], axis=0)


def _router_tail(i, x, y, mod, g_ffn, wr, br, xo_ref, h_ref, ri_ref, rg_ref, cnt_ref, carry_ref):
    gate_mix = mod[:, 2 * D_MODEL:3 * D_MODEL]
    shift = mod[:, 3 * D_MODEL:4 * D_MODEL]
    scale = mod[:, 4 * D_MODEL:5 * D_MODEL]
    xn = x + gate_mix * y
    xo_ref[...] = xn
    h = _rms(xn, g_ffn) * (1.0 + scale) + shift
    h_ref[...] = h
    h_hi = h.astype(BF16)
    h_lo = (h - h_hi.astype(F32)).astype(BF16)
    w_hi = wr.astype(BF16)
    w_lo = (wr - w_hi.astype(F32)).astype(BF16)
    logits = _dot(h_hi, w_hi) + _dot(h_lo, w_hi) + _dot(h_hi, w_lo) + br
    lane = _lane_iota(logits.shape)
    lane_f = lane.astype(F32)
    vals, idxs = [], []
    l = logits
    for _ in range(TOP_K):
        m = l.max(axis=-1, keepdims=True)
        idx = jnp.where(l == m, lane_f, float(LANES)).min(axis=-1, keepdims=True)
        vals.append(m)
        idxs.append(idx)
        l = jnp.where(lane_f == idx, -jnp.inf, l)
    es = [jnp.exp(v - vals[0]) for v in vals]
    den = es[0]
    for e in es[1:]:
        den = den + e
    sel = jnp.zeros(logits.shape, F32)
    for idx in idxs:
        sel = sel + jnp.where(lane_f == idx, 1.0, 0.0)

    @pl.when(i == 0)
    def _():
        carry_ref[...] = jnp.zeros_like(carry_ref)

    carry = carry_ref[...]
    r_io = lax.broadcasted_iota(I32, (TM, TM), 0)
    c_io = lax.broadcasted_iota(I32, (TM, TM), 1)
    tri = jnp.where(c_io < r_io, 1.0, 0.0).astype(BF16)
    rank = _dot(tri, sel.astype(BF16)) + carry
    carry = carry + sel.sum(axis=0, keepdims=True)
    carry_ref[...] = carry
    cnt_ref[...] = carry
    ri = jnp.zeros(logits.shape, F32)
    rg = jnp.zeros(logits.shape, F32)
    for k in range(TOP_K):
        rk = jnp.where(lane_f == idxs[k], rank, 0.0).sum(axis=-1, keepdims=True)
        ri = jnp.where(lane == k, idxs[k], ri)
        ri = jnp.where(lane == TOP_K + k, rk, ri)
        rg = jnp.where(lane == k, es[k] / den, rg)
    ri_ref[...] = ri.astype(I32)
    rg_ref[...] = rg


def _tail_specs(t):
    row = lambda w: pl.BlockSpec((TM, w), lambda i: (i, 0))
    out_specs = [row(D_MODEL), row(D_MODEL), row(LANES), row(LANES),
                 pl.BlockSpec((1, LANES), lambda i: (0, 0))]
    out_shape = [jax.ShapeDtypeStruct((t, D_MODEL), F32), jax.ShapeDtypeStruct((t, D_MODEL), F32),
                 jax.ShapeDtypeStruct((t, LANES), I32), jax.ShapeDtypeStruct((t, LANES), F32),
                 jax.ShapeDtypeStruct((1, LANES), F32)]
    return out_specs, out_shape


def _post_attn_kernel(x_ref, o_ref, mod_ref, w_ref, g_ref, wr_ref, br_ref,
                      xo_ref, h_ref, ri_ref, rg_ref, cnt_ref, carry_ref):
    i = pl.program_id(0)
    y = _dot(o_ref[...], w_ref[...])
    _router_tail(i, x_ref[...], y, mod_ref[0], g_ref[...], wr_ref[...], br_ref[...],
                 xo_ref, h_ref, ri_ref, rg_ref, cnt_ref, carry_ref)


def _post_attn_call(x, o, mod3, layer, w_out_bf, g_ffn, wr, br, npt, tps):
    t = x.shape[0]
    row = lambda w: pl.BlockSpec((TM, w), lambda i: (i, 0))
    full = lambda a: pl.BlockSpec(a.shape, lambda i: (0,) * a.ndim)
    out_specs, out_shape = _tail_specs(t)
    return pl.pallas_call(
        _post_attn_kernel,
        grid=(t // TM,),
        in_specs=[row(D_MODEL), row(D_MODEL),
                  pl.BlockSpec((1, 1, N_MOD * D_MODEL), lambda i: (layer * 3 + _cond_row(i, npt, tps), 0, 0)),
                  full(w_out_bf), full(g_ffn), full(wr), full(br)],
        out_specs=out_specs, out_shape=out_shape,
        scratch_shapes=[pltpu.VMEM((1, LANES), F32)],
        compiler_params=_cparams(1),
    )(x, o, mod3, w_out_bf, g_ffn, wr, br)


def _pool_kernel(npt, tps, x_ref, xp_ref, xn_ref, mod_ref, gm_ref, wp_ref, ps_ref, g_ref, wr_ref, br_ref,
                 xo_ref, h_ref, ri_ref, rg_ref, cnt_ref, carry_ref):
    i = pl.program_id(0)
    mod = mod_ref[0]
    shift = mod[:, 0:D_MODEL]
    scale = mod[:, D_MODEL:2 * D_MODEL]
    gm = gm_ref[...]
    x = x_ref[...]
    pre = lambda v: _rms(v, gm) * (1.0 + scale) + shift
    j = jnp.where(i < npt, 0, (i - npt) % tps)
    ntile = jnp.where(i < npt, 1, tps)
    has_prev = (j > 0).astype(F32)
    has_next = (j < ntile - 1).astype(F32)
    h = pre(x)
    hc = jnp.concatenate([pre(xp_ref[...]) * has_prev, h, pre(xn_ref[...]) * has_next], axis=0)
    rows = hc.shape[0]
    t_seq = (j * TM + lax.broadcasted_iota(I32, (TM, 1), 0)).astype(F32)
    seq_len = (ntile * TM).astype(F32)
    ys = []
    for g, w in enumerate(POOL_WINDOWS):
        half = w // 2
        s = hc[:, g * POOL_CH:(g + 1) * POOL_CH]
        step = 1
        while step < w:
            s = s + pltpu.roll(s, rows - step, 0)
            step *= 2
        s = pltpu.roll(s, half, 0) if half != HALO else s
        win = s[HALO:HALO + TM] if half != HALO else s[0:TM]
        cnt = jnp.minimum(t_seq + half, seq_len) - jnp.maximum(t_seq - half, 0.0)
        pooled = win / cnt - h[:, g * POOL_CH:(g + 1) * POOL_CH]
        ys.append(_dot(pooled.astype(BF16), wp_ref[g]))
    y = jnp.concatenate(ys, axis=1) * ps_ref[...]
    _router_tail(i, x, y, mod, g_ref[...], wr_ref[...], br_ref[...],
                 xo_ref, h_ref, ri_ref, rg_ref, cnt_ref, carry_ref)


def _pool_call(x, mod3, layer, g_mix, w_pool_bf, pool_scale, g_ffn, wr, br, npt, tps):
    t = x.shape[0]
    per = TM // HALO
    nh = t // HALO
    row = lambda w: pl.BlockSpec((TM, w), lambda i: (i, 0))
    full = lambda a: pl.BlockSpec(a.shape, lambda i: (0,) * a.ndim)
    out_specs, out_shape = _tail_specs(t)
    return pl.pallas_call(
        functools.partial(_pool_kernel, npt, tps),
        grid=(t // TM,),
        in_specs=[row(D_MODEL),
                  pl.BlockSpec((HALO, D_MODEL), lambda i: (jnp.maximum(i * per - 1, 0), 0)),
                  pl.BlockSpec((HALO, D_MODEL), lambda i: (jnp.minimum((i + 1) * per, nh - 1), 0)),
                  pl.BlockSpec((1, 1, N_MOD * D_MODEL), lambda i: (layer * 3 + _cond_row(i, npt, tps), 0, 0)),
                  full(g_mix), full(w_pool_bf), full(pool_scale), full(g_ffn), full(wr), full(br)],
        out_specs=out_specs, out_shape=out_shape,
        scratch_shapes=[pltpu.VMEM((1, LANES), F32)],
        compiler_params=_cparams(1),
    )(x, x, x, mod3, g_mix, w_pool_bf, pool_scale, g_ffn, wr, br)


def _dispatch_kernel(pos_ref, h_ref, xs_ref, sem):
    i = pl.program_id(0)

    def issue(r, c):
        base = (i * TM + r) * TOP_K
        for k in range(TOP_K):
            pltpu.make_async_copy(h_ref.at[pl.ds(r, 1), :],
                                  xs_ref.at[pl.ds(pos_ref[base + k], 1), :], sem).start()
        return c

    lax.fori_loop(0, TM, issue, 0)
    for k in range(TOP_K):
        pltpu.make_async_copy(h_ref, xs_ref.at[pl.ds(0, TM), :], sem).wait()


def _dispatch_call(pos, h):
    t = h.shape[0]
    return pl.pallas_call(
        _dispatch_kernel,
        grid_spec=pltpu.PrefetchScalarGridSpec(
            num_scalar_prefetch=1, grid=(t // TM,),
            in_specs=[pl.BlockSpec((TM, D_MODEL), lambda i, pos: (i, 0))],
            out_specs=pl.BlockSpec(memory_space=pl.ANY),
            scratch_shapes=[pltpu.SemaphoreType.DMA(())]),
        out_shape=jax.ShapeDtypeStruct((t * TOP_K, D_MODEL), F32),
        compiler_params=_cparams(1),
    )(pos, h)


def _gmm_kernel(wt_ref, we_ref, lo_ref, hi_ref, first_ref, x_ref, wgu_ref, bgu_ref, wd_ref, bd_ref, y_ref):
    w = pl.program_id(0)
    lo = lo_ref[w]
    hi = hi_ref[w]

    @pl.when(hi > lo)
    def _():
        x = x_ref[...].astype(BF16)
        acc = jnp.zeros((TM, D_MODEL), F32)
        for c in range(0, D_EXPERT, FF_CHUNK):
            gate = _dot(x, wgu_ref[:, c:c + FF_CHUNK].astype(BF16)) + bgu_ref[0][:, c:c + FF_CHUNK]
            up = (_dot(x, wgu_ref[:, D_EXPERT + c:D_EXPERT + c + FF_CHUNK].astype(BF16))
                  + bgu_ref[0][:, D_EXPERT + c:D_EXPERT + c + FF_CHUNK])
            gate = jnp.minimum(gate, SWIGLU_LIMIT)
            up = jnp.clip(up, -SWIGLU_LIMIT, SWIGLU_LIMIT)
            act = gate * jax.nn.sigmoid(SWIGLU_ALPHA * gate) * (up + 1.0)
            acc = acc + _dot(act.astype(BF16), wd_ref[c:c + FF_CHUNK, :].astype(BF16))
        y = acc + bd_ref[0]

        @pl.when(first_ref[w] == 1)
        def _():
            y_ref[...] = y

        @pl.when(first_ref[w] == 0)
        def _():
            r = lax.broadcasted_iota(I32, (TM, 1), 0)
            y_ref[...] = jnp.where((r >= lo) & (r < hi), y, y_ref[...])


def _gmm_call(wt, we, lo, hi, first, xs, w_gu, b_gu, w_dn, b_dn, layer):
    r = xs.shape[0]
    nw = wt.shape[0]
    return pl.pallas_call(
        _gmm_kernel,
        grid_spec=pltpu.PrefetchScalarGridSpec(
            num_scalar_prefetch=5, grid=(nw,),
            in_specs=[
                pl.BlockSpec((TM, D_MODEL), lambda w, wt, we, lo, hi, fi: (wt[w], 0)),
                pl.BlockSpec((None, None, D_MODEL, 2 * D_EXPERT), lambda w, wt, we, lo, hi, fi: (layer, we[w], 0, 0)),
                pl.BlockSpec((1, 1, 2 * D_EXPERT), lambda w, wt, we, lo, hi, fi: (layer * N_EXPERTS + we[w], 0, 0)),
                pl.BlockSpec((None, None, D_EXPERT, D_MODEL), lambda w, wt, we, lo, hi, fi: (layer, we[w], 0, 0)),
                pl.BlockSpec((1, 1, D_MODEL), lambda w, wt, we, lo, hi, fi: (layer * N_EXPERTS + we[w], 0, 0)),
            ],
            out_specs=pl.BlockSpec((TM, D_MODEL), lambda w, wt, we, lo, hi, fi: (wt[w], 0))),
        out_shape=jax.ShapeDtypeStruct((r, D_MODEL), F32),
        compiler_params=_cparams(1),
    )(wt, we, lo, hi, first, xs, w_gu, b_gu, w_dn, b_dn)


def _combine_kernel(final, npt, pos_ref, ys_ref, x_ref, rg_ref, mod_ref, gf_ref, *rest):
    if final:
        xo_ref, yo_ref, buf, sem = rest
    else:
        xo_ref, buf, sem = rest
    i = pl.program_id(0)

    def issue(r, c):
        base = (i * TM + r) * TOP_K
        for k in range(TOP_K):
            pltpu.make_async_copy(ys_ref.at[pl.ds(pos_ref[base + k], 1), :],
                                  buf.at[k, pl.ds(r, 1), :], sem).start()
        return c

    lax.fori_loop(0, TM, issue, 0)
    for k in range(TOP_K):
        pltpu.make_async_copy(ys_ref.at[pl.ds(0, TM), :], buf.at[k], sem).wait()
    rg = rg_ref[...]
    moe = rg[:, 0:1] * buf[0]
    for k in range(1, TOP_K):
        moe = moe + rg[:, k:k + 1] * buf[k]
    gate_ffn = mod_ref[0][:, 5 * D_MODEL:6 * D_MODEL]
    xn = x_ref[...] + gate_ffn * moe
    xo_ref[...] = xn
    if final:
        yo_ref[...] = _rms(xn, gf_ref[...])


def _combine_call(pos, ys, x, rg, mod3, layer, g_final, final, npt, tps):
    t = x.shape[0]
    row = lambda w: pl.BlockSpec((TM, w), lambda i, pos: (i, 0))
    n_out = 2 if final else 1
    return pl.pallas_call(
        functools.partial(_combine_kernel, final, npt),
        grid_spec=pltpu.PrefetchScalarGridSpec(
            num_scalar_prefetch=1, grid=(t // TM,),
            in_specs=[pl.BlockSpec(memory_space=pl.ANY), row(D_MODEL), row(LANES),
                      pl.BlockSpec((1, 1, N_MOD * D_MODEL),
                                   lambda i, pos: (layer * 3 + _cond_row(i, npt, tps), 0, 0)),
                      pl.BlockSpec(g_final.shape, lambda i, pos: (0, 0))],
            out_specs=[row(D_MODEL)] * n_out,
            scratch_shapes=[pltpu.VMEM((TOP_K, TM, D_MODEL), F32), pltpu.SemaphoreType.DMA(())]),
        out_shape=[jax.ShapeDtypeStruct((t, D_MODEL), F32)] * n_out,
        compiler_params=_cparams(1),
    )(pos, ys, x, rg, mod3, g_final)


def _routing_tables(ri, cnt, n_rows):
    counts = cnt[0, :N_EXPERTS].astype(I32)
    ends = jnp.cumsum(counts)
    starts = ends - counts
    e_idx = ri[:, :TOP_K]
    rank = ri[:, TOP_K:2 * TOP_K]
    onehot = e_idx[:, :, None] == jnp.arange(N_EXPERTS, dtype=I32)[None, None, :]
    pos = rank + jnp.sum(jnp.where(onehot, starts[None, None, :], 0), axis=-1)
    pos = pos.reshape(-1).astype(I32)
    n_tiles = n_rows // TM
    nw = n_tiles + N_EXPERTS
    first_tile = starts // TM
    last_tile = jnp.maximum(ends - 1, 0) // TM
    n_items = jnp.where(counts > 0, last_tile - first_tile + 1, 0)
    item_end = jnp.cumsum(n_items)
    item_start = item_end - n_items
    total = item_end[-1]
    w = jnp.arange(nw, dtype=I32)
    wc = jnp.minimum(w, total - 1)
    e_w = jnp.sum((item_end[None, :] <= wc[:, None]).astype(I32), axis=1)
    tile_w = first_tile[e_w] + (wc - item_start[e_w])
    lo = jnp.maximum(starts[e_w], tile_w * TM) - tile_w * TM
    hi = jnp.minimum(ends[e_w], (tile_w + 1) * TM) - tile_w * TM
    real = w < total
    lo = jnp.where(real, lo, 0)
    hi = jnp.where(real, hi, 0)
    first = jnp.where(real & (lo == 0), 1, 0)
    return pos, tile_w.astype(I32), e_w.astype(I32), lo.astype(I32), hi.astype(I32), first.astype(I32)


def _moe(x, h, ri, rg, cnt, mod3, layer, w_gu, b_gu, w_dn, b_dn, g_final, final, npt, tps):
    t = x.shape[0]
    pos, wt, we, lo, hi, first = _routing_tables(ri, cnt, t * TOP_K)
    xs = _dispatch_call(pos, h)
    ys = _gmm_call(wt, we, lo, hi, first, xs, w_gu, b_gu, w_dn, b_dn, layer)
    return _combine_call(pos, ys, x, rg, mod3, layer, g_final, final, npt, tps)


def _rope_tables(n_tokens):
    n_rows = n_tokens // GRID_W
    rows = jnp.repeat(jnp.arange(n_rows, dtype=F32), GRID_W)
    cols = jnp.tile(jnp.arange(GRID_W, dtype=F32), n_rows)
    axis_dim = A_HEAD_DIM // 2
    inv = ROPE_THETA ** (-jnp.arange(0, axis_dim, 2, dtype=F32) / axis_dim)
    ang_r = rows[:, None] * inv[None, :]
    ang_c = cols[:, None] * inv[None, :]
    ang = jnp.concatenate([ang_r, ang_r, ang_c, ang_c], axis=-1)
    cos, sin = jnp.cos(ang), jnp.sin(ang)
    sign = jnp.where((jnp.arange(A_HEAD_DIM) % 32) < 16, -1.0, 1.0).astype(F32)
    sin = sin * sign[None, :]
    cos = jnp.concatenate([jnp.ones((TM, A_HEAD_DIM), F32), cos], axis=0)
    sin = jnp.concatenate([jnp.zeros((TM, A_HEAD_DIM), F32), sin], axis=0)
    return jnp.tile(cos, (1, LANES // A_HEAD_DIM)), jnp.tile(sin, (1, LANES // A_HEAD_DIM))


def kernel(x_prompt, x_sample, cache_diff_k, cache_diff_v, cache_gqa_k, cache_gqa_v, c, c_ctx, w_ada, b_ada, norm_mix, norm_ffn, norm_final, w_attn_in, w_attn_out, lam_q1, lam_k1, lam_q2, lam_k2, diff_subln, gqa_q_norm, gqa_k_norm, w_pool, pool_scale, w_router, b_router, w_gate_up, b_gate_up, w_down, b_down):
    n_prompt, prompt_len, d = x_prompt.shape
    n_sample, sample_len, _ = x_sample.shape
    assert d == D_MODEL and prompt_len % TM == 0 and sample_len % TM == 0
    assert n_sample + 1 <= SUBLANES
    npt = n_prompt * prompt_len // TM
    tps = sample_len // TM
    tp = n_prompt * prompt_len
    past = cache_diff_k.shape[2]

    x = jnp.concatenate([x_prompt.reshape(tp, d), x_sample.reshape(n_sample * sample_len, d)], axis=0)
    cond8 = jnp.zeros((SUBLANES, d), F32).at[0].set(c_ctx).at[1:1 + n_sample].set(c)
    mod = _modulation_all(cond8, w_ada, b_ada)
    mod3 = mod[:, :1 + n_sample].reshape(DEPTH * (1 + n_sample), 1, N_MOD * d)
    assert n_sample == 2

    cos_t, sin_t = _rope_tables(sample_len)
    seg_r = jnp.arange(BQ_W)[:, None] // B_HEAD_DIM
    seg = (seg_r == seg_r.T).astype(BF16)
    b_gu = b_gate_up.reshape(DEPTH * N_EXPERTS, 1, 2 * D_EXPERT)
    b_dn = b_down.reshape(DEPTH * N_EXPERTS, 1, D_MODEL)

    caches = []
    y_final = None
    for i in range(DEPTH):
        j = i // 2
        g_mix = norm_mix[i][None]
        g_ffn = norm_ffn[i][None]
        wr = jnp.zeros((d, LANES), F32).at[:, :N_EXPERTS].set(w_router[i])
        br = jnp.full((1, LANES), NEG_BIG, F32).at[0, :N_EXPERTS].set(b_router[i])
        if i % 2 == 0:
            lambda_init = 0.8 - 0.6 * math.exp(-0.3 * i)
            gq = jnp.tile(gqa_q_norm[j], BQ_W // B_HEAD_DIM)[None]
            gk = jnp.tile(gqa_k_norm[j], BKV_W // B_HEAD_DIM)[None]
            qa, ka, va, qb, kbd, vb, ck, cv, cgk, cgv = _qkv_call(
                x, mod3, i, g_mix, w_attn_in[j].astype(BF16), gq, gk, cos_t, sin_t, seg, npt, tps)
            caches.append((ck, cv, cgk, cgv))
            cka = cache_diff_k[:, j].reshape(n_sample, past, A_W).astype(BF16)
            cva = cache_diff_v[:, j].reshape(n_sample, past, A_W).astype(BF16)
            gk_c = cache_gqa_k[:, j]
            ckbd = jnp.concatenate([gk_c[:, :, 0], gk_c[:, :, 0], gk_c[:, :, 1], gk_c[:, :, 1]],
                                   axis=-1).astype(BF16)
            cvb = cache_gqa_v[:, j].reshape(n_sample, past, BKV_W).astype(BF16)
            o = _attention(lambda_init, qa, ka, va, qb, kbd, vb, cka, cva, ckbd, cvb,
                           lam_q1[j][None], lam_k1[j][None], lam_q2[j][None], lam_k2[j][None],
                           diff_subln[j][None], n_prompt, prompt_len, n_sample, sample_len)
            x, h, ri, rg, cnt = _post_attn_call(x, o, mod3, i, w_attn_out[j].astype(BF16), g_ffn, wr, br, npt, tps)
        else:
            x, h, ri, rg, cnt = _pool_call(x, mod3, i, g_mix, w_pool[j].astype(BF16), pool_scale[j][None],
                                           g_ffn, wr, br, npt, tps)
        final = i == DEPTH - 1
        outs = _moe(x, h, ri, rg, cnt, mod3, i, w_gate_up, b_gu, w_down, b_dn, norm_final[None], final, npt, tps)
        x = outs[0]
        if final:
            y_final = outs[1]

    y_prompt = y_final[:tp].reshape(n_prompt, prompt_len, d)
    y_sample = y_final[tp:].reshape(n_sample, sample_len, d)
    stack = lambda k, shp: jnp.stack([cc[k].reshape(shp) for cc in caches], axis=1)
    new_diff_k = stack(0, (n_prompt, prompt_len, A_HEADS, 2 * A_HEAD_DIM))
    new_diff_v = stack(1, (n_prompt, prompt_len, A_HEADS, 2 * A_HEAD_DIM))
    new_gqa_k = stack(2, (n_prompt, prompt_len, B_KV_HEADS, B_HEAD_DIM))
    new_gqa_v = stack(3, (n_prompt, prompt_len, B_KV_HEADS, B_HEAD_DIM))
    return (y_prompt, y_sample, new_diff_k, new_diff_v, new_gqa_k, new_gqa_v)
```

```python
import functools
import math

import jax
import jax.numpy as jnp
from jax import lax
from jax.experimental import pallas as pl
from jax.experimental.pallas import tpu as pltpu

F32 = jnp.float32
BF16 = jnp.bfloat16
I32 = jnp.int32

D_MODEL = 1024
DEPTH = 4
GRID_W = 64
A_HEADS = 4
A_HEAD_DIM = 64
B_Q_HEADS = 8
B_KV_HEADS = 2
B_HEAD_DIM = 64
ROPE_THETA = 10000.0
A_W = A_HEADS * 2 * A_HEAD_DIM
BQ_W = B_Q_HEADS * B_HEAD_DIM
BKV_W = B_KV_HEADS * B_HEAD_DIM
IN_W = 3 * A_W + BQ_W + 2 * BKV_W
POOL_WINDOWS = (2, 4, 8, 16)
POOL_GROUPS = 4
POOL_CH = D_MODEL // POOL_GROUPS
N_EXPERTS = 32
TOP_K = 4
D_EXPERT = D_MODEL
SWIGLU_LIMIT = 7.0
SWIGLU_ALPHA = 1.702
N_MOD = 6
EPS = 1e-6

LANES = 128
SUBLANES = 8
TM = 256
HALO = 8
ADA_TN = 1536
FF_CHUNK = 512
VMEM_LIMIT = 56 * 1024 * 1024
NEG_BIG = -1e30


def _cparams(n_axes, vmem=VMEM_LIMIT):
    return pltpu.CompilerParams(
        dimension_semantics=("arbitrary",) * n_axes, vmem_limit_bytes=vmem)


def _dot(a, b):
    return jnp.dot(a, b, preferred_element_type=F32)


def _dot_nt(a, b):
    return lax.dot_general(a, b, (((1,), (1,)), ((), ())), preferred_element_type=F32)


def _rms(x, g):
    ms = jnp.mean(x * x, axis=-1, keepdims=True)
    return x * lax.rsqrt(ms + EPS) * g


def _lane_iota(shape):
    return lax.broadcasted_iota(I32, shape, len(shape) - 1)


def _ada_kernel(c_ref, w_ref, b_ref, o_ref):
    c = c_ref[...]
    s = (c * jax.nn.sigmoid(c)).astype(BF16)
    o_ref[0] = _dot(s, w_ref[0].astype(BF16)) + b_ref[0]


def _modulation_all(cond8, w_ada, b_ada):
    nmod = w_ada.shape[-1]
    return pl.pallas_call(
        _ada_kernel,
        grid=(DEPTH, nmod // ADA_TN),
        in_specs=[
            pl.BlockSpec((SUBLANES, D_MODEL), lambda l, n: (0, 0)),
            pl.BlockSpec((1, D_MODEL, ADA_TN), lambda l, n: (l, 0, n)),
            pl.BlockSpec((1, 1, ADA_TN), lambda l, n: (l, 0, n)),
        ],
        out_specs=pl.BlockSpec((1, SUBLANES, ADA_TN), lambda l, n: (l, 0, n)),
        out_shape=jax.ShapeDtypeStruct((DEPTH, SUBLANES, nmod), F32),
        compiler_params=_cparams(2),
        name="modulation",
    )(cond8, w_ada, b_ada.reshape(DEPTH, 1, nmod))


def _seg_meansq(x, seg):
    x2 = x * x
    hi = x2.astype(BF16)
    lo = (x2 - hi.astype(F32)).astype(BF16)
    return (_dot(hi, seg) + _dot(lo, seg)) * (1.0 / B_HEAD_DIM)


def _rope128(x, cos, sin_signed):
    lo16 = (_lane_iota(x.shape) % 32) < 16
    nxt = pltpu.roll(x, LANES - 16, 1)
    prv = pltpu.roll(x, 16, 1)
    return x * cos + jnp.where(lo16, nxt, prv) * sin_signed


def _rope(x, cos, sin_signed):
    cols = [_rope128(x[:, c:c + LANES], cos, sin_signed) for c in range(0, x.shape[1], LANES)]
    return cols[0] if len(cols) == 1 else jnp.concatenate(cols, axis=1)


def _qkv_kernel(n_prompt_tiles, x_ref, mod_ref, g_ref, w_ref, gq_ref, gk_ref, cos_ref, sin_ref,
                seg_ref, qa_ref, ka_ref, va_ref, qb_ref, kbd_ref, vb_ref,
                ck_ref, cv_ref, cgk_ref, cgv_ref):
    i = pl.program_id(0)
    mod = mod_ref[0]
    shift = mod[:, 0:D_MODEL]
    scale = mod[:, D_MODEL:2 * D_MODEL]
    h = _rms(x_ref[...], g_ref[...]) * (1.0 + scale) + shift
    p = _dot(h.astype(BF16), w_ref[...])
    a_q = p[:, 0:A_W]
    a_k = p[:, A_W:2 * A_W]
    a_v = p[:, 2 * A_W:3 * A_W]
    o = 3 * A_W
    b_q = p[:, o:o + BQ_W]
    b_k = p[:, o + BQ_W:o + BQ_W + BKV_W]
    b_v = p[:, o + BQ_W + BKV_W:o + BQ_W + 2 * BKV_W]
    seg = seg_ref[...]
    b_q = b_q * lax.rsqrt(_seg_meansq(b_q, seg) + EPS) * gq_ref[...]
    b_k = b_k * lax.rsqrt(_seg_meansq(b_k, seg[:BKV_W, :BKV_W]) + EPS) * gk_ref[...]

    @pl.when(i < n_prompt_tiles)
    def _():
        ck_ref[...] = a_k
        cv_ref[...] = a_v
        cgk_ref[...] = b_k
        cgv_ref[...] = b_v

    cos = cos_ref[...]
    sin = sin_ref[...]
    sm = A_HEAD_DIM ** -0.5
    qa_ref[...] = (_rope(a_q, cos, sin) * sm).astype(BF16)
    ka_ref[...] = _rope(a_k, cos, sin).astype(BF16)
    va_ref[...] = a_v.astype(BF16)
    qb_ref[...] = (_rope(b_q, cos, sin) * (B_HEAD_DIM ** -0.5)).astype(BF16)
    kb = _rope(b_k, cos, sin)
    kb_sw = pltpu.roll(kb, B_HEAD_DIM, 1)
    lo = _lane_iota(kb.shape) < B_HEAD_DIM
    kbd_ref[...] = jnp.concatenate(
        [jnp.where(lo, kb, kb_sw), jnp.where(lo, kb_sw, kb)], axis=1).astype(BF16)
    vb_ref[...] = b_v.astype(BF16)


def _cond_row(i, npt, tiles_per_seq):
    return jnp.where(i < npt, 0, 1 + (i - npt) // tiles_per_seq)


def _qkv_call(x, mod3, layer, g_mix, w_in_bf, gq, gk, cos_t, sin_t, seg, npt, tps):
    t = x.shape[0]
    nt = t // TM
    tp = npt * TM
    row = lambda w: pl.BlockSpec((TM, w), lambda i: (i, 0))
    full = lambda a: pl.BlockSpec(a.shape, lambda i: (0,) * a.ndim)
    tab = pl.BlockSpec((TM, LANES), lambda i: (jnp.where(i < npt, 0, 1 + (i - npt) % tps), 0))
    cache = lambda w: pl.BlockSpec((TM, w), lambda i: (jnp.minimum(i, npt - 1), 0))
    return pl.pallas_call(
        functools.partial(_qkv_kernel, npt),
        grid=(nt,),
        in_specs=[
            row(D_MODEL),
            pl.BlockSpec((1, 1, N_MOD * D_MODEL), lambda i: (layer * 3 + _cond_row(i, npt, tps), 0, 0)),
            full(g_mix), full(w_in_bf), full(gq), full(gk), tab, tab, full(seg),
        ],
        out_specs=[row(A_W), row(A_W), row(A_W), row(BQ_W), row(2 * BKV_W), row(BKV_W),
                   cache(A_W), cache(A_W), cache(BKV_W), cache(BKV_W)],
        out_shape=[
            jax.ShapeDtypeStruct((t, A_W), BF16), jax.ShapeDtypeStruct((t, A_W), BF16),
            jax.ShapeDtypeStruct((t, A_W), BF16), jax.ShapeDtypeStruct((t, BQ_W), BF16),
            jax.ShapeDtypeStruct((t, 2 * BKV_W), BF16), jax.ShapeDtypeStruct((t, BKV_W), BF16),
            jax.ShapeDtypeStruct((tp, A_W), F32), jax.ShapeDtypeStruct((tp, A_W), F32),
            jax.ShapeDtypeStruct((tp, BKV_W), F32), jax.ShapeDtypeStruct((tp, BKV_W), F32),
        ],
        compiler_params=_cparams(1),
        name="attn_qkv",
    )(x, mod3, g_mix, w_in_bf, gq, gk, cos_t, sin_t, seg)


def _softmax_pv(qq, ks, vs):
    ss = [_dot_nt(qq, k) for k in ks]
    m = ss[0].max(axis=-1, keepdims=True)
    for s in ss[1:]:
        m = jnp.maximum(m, s.max(axis=-1, keepdims=True))
    acc = None
    l = None
    for s, v in zip(ss, vs):
        p = jnp.exp(s - m)
        ls = p.sum(axis=-1, keepdims=True)
        pv = _dot(p.astype(BF16), v)
        acc = pv if acc is None else acc + pv
        l = ls if l is None else l + ls
    return acc / l


def _split_halves(q):
    lo = _lane_iota(q.shape) < (LANES // 2)
    zero = jnp.zeros_like(q)
    return jnp.concatenate([jnp.where(lo, q, zero), jnp.where(lo, zero, q)], axis=0)


def _attn_body(lambda_init, qa, qb, ka_segs, va_segs, kbd_segs, vb_segs, lam, subln):
    tq = qa.shape[0]
    cols = []
    for h in range(A_HEADS):
        sl = slice(h * LANES, (h + 1) * LANES)
        o = _softmax_pv(_split_halves(qa[:, sl]), [k[:, sl] for k in ka_segs],
                        [v[:, sl] for v in va_segs])
        od = o[:tq] - lam * o[tq:]
        cols.append(_rms(od, subln) * (1.0 - lambda_init))
    lo = _lane_iota((tq, LANES)) < B_HEAD_DIM
    for c in range(BQ_W // LANES):
        g = (2 * c) // (B_Q_HEADS // B_KV_HEADS)
        sl = slice(c * LANES, (c + 1) * LANES)
        gl = slice(g * LANES, (g + 1) * LANES)
        o = _softmax_pv(_split_halves(qb[:, sl]), [k[:, gl] for k in kbd_segs], vb_segs)
        oe, oo = o[:tq], o[tq:]
        if g == 0:
            cols.append(jnp.where(lo, oe, pltpu.roll(oo, B_HEAD_DIM, 1)))
        else:
            cols.append(jnp.where(lo, pltpu.roll(oe, B_HEAD_DIM, 1), oo))
    return jnp.concatenate(cols, axis=1).astype(BF16)


def _lambda(lq1, lk1, lq2, lk2, lambda_init):
    return (jnp.exp(jnp.sum(lq1 * lk1, axis=-1, keepdims=True))
            - jnp.exp(jnp.sum(lq2 * lk2, axis=-1, keepdims=True)) + lambda_init)


def _attn_prompt_kernel(lambda_init, qa, ka, va, qb, kbd, vb, lq1, lk1, lq2, lk2, subln, o_ref):
    lam = _lambda(lq1[...], lk1[...], lq2[...], lk2[...], lambda_init)
    o_ref[...] = _attn_body(lambda_init, qa[...], qb[...], [ka[...]], [va[...]],
                            [kbd[...]], [vb[...]], lam, subln[...])


def _attn_sample_kernel(lambda_init, qa, ka, va, qb, kbd, vb, cka, cva, ckbd, cvb,
                        lq1, lk1, lq2, lk2, subln, o_ref):
    lam = _lambda(lq1[...], lk1[...], lq2[...], lk2[...], lambda_init)
    o_ref[...] = _attn_body(lambda_init, qa[...], qb[...], [ka[...], cka[...]], [va[...], cva[...]],
                            [kbd[...], ckbd[...]], [vb[...], cvb[...]], lam, subln[...])


def _attention(lambda_init, qa, ka, va, qb, kbd, vb, cka, cva, ckbd, cvb,
               lq1, lk1, lq2, lk2, subln, n_prompt, prompt_len, n_sample, sample_len):
    npt = n_prompt * prompt_len // TM
    small = [lq1, lk1, lq2, lk2, subln]
    widths = [A_W, A_W, A_W, BQ_W, 2 * BKV_W, BKV_W]
    assert prompt_len == TM
    full1 = lambda a: pl.BlockSpec(a.shape, lambda b: (0,) * a.ndim)
    o_p = pl.pallas_call(
        functools.partial(_attn_prompt_kernel, lambda_init),
        grid=(n_prompt,),
        in_specs=[pl.BlockSpec((TM, w), lambda b: (b, 0)) for w in widths] + [full1(a) for a in small],
        out_specs=pl.BlockSpec((TM, D_MODEL), lambda b: (b, 0)),
        out_shape=jax.ShapeDtypeStruct((npt * TM, D_MODEL), BF16),
        compiler_params=_cparams(1),
        name="attn_prompt",
    )(qa, ka, va, qb, kbd, vb, *small)
    tps = sample_len // TM
    seq0 = n_prompt * prompt_len // sample_len
    assert seq0 * sample_len == n_prompt * prompt_len
    past = cka.shape[1]
    qspec = lambda w: pl.BlockSpec((TM, w), lambda b, q: (npt + b * tps + q, 0))
    kspec = lambda w: pl.BlockSpec((sample_len, w), lambda b, q: (seq0 + b, 0))
    cspec = lambda w: pl.BlockSpec((None, past, w), lambda b, q: (b, 0, 0))
    full2 = lambda a: pl.BlockSpec(a.shape, lambda b, q: (0,) * a.ndim)
    o_s = pl.pallas_call(
        functools.partial(_attn_sample_kernel, lambda_init),
        grid=(n_sample, tps),
        in_specs=[qspec(A_W), kspec(A_W), kspec(A_W), qspec(BQ_W), kspec(2 * BKV_W), kspec(BKV_W),
                  cspec(A_W), cspec(A_W), cspec(2 * BKV_W), cspec(BKV_W)] + [full2(a) for a in small],
        out_specs=pl.BlockSpec((TM, D_MODEL), lambda b, q: (b * tps + q, 0)),
        out_shape=jax.ShapeDtypeStruct((n_sample * sample_len, D_MODEL), BF16),
        compiler_params=_cparams(2),
        name="attn_sample",
    )(qa, ka, va, qb, kbd, vb, cka, cva, ckbd, cvb, *small)
    return jnp.concatenate([o_p, o_s], axis=0)


def _router_tail(i, x, y, mod, g_ffn, wr, br, xo_ref, h_ref, ri_ref, rg_ref, cnt_ref, carry_ref):
    gate_mix = mod[:, 2 * D_MODEL:3 * D_MODEL]
    shift = mod[:, 3 * D_MODEL:4 * D_MODEL]
    scale = mod[:, 4 * D_MODEL:5 * D_MODEL]
    xn = x + gate_mix * y
    xo_ref[...] = xn
    h = _rms(xn, g_ffn) * (1.0 + scale) + shift
    h_ref[...] = h
    h_hi = h.astype(BF16)
    h_lo = (h - h_hi.astype(F32)).astype(BF16)
    w_hi = wr.astype(BF16)
    w_lo = (wr - w_hi.astype(F32)).astype(BF16)
    logits = _dot(h_hi, w_hi) + _dot(h_lo, w_hi) + _dot(h_hi, w_lo) + br
    lane = _lane_iota(logits.shape)
    lane_f = lane.astype(F32)
    vals, idxs = [], []
    l = logits
    for _ in range(TOP_K):
        m = l.max(axis=-1, keepdims=True)
        idx = jnp.where(l == m, lane_f, float(LANES)).min(axis=-1, keepdims=True)
        vals.append(m)
        idxs.append(idx)
        l = jnp.where(lane_f == idx, -jnp.inf, l)
    es = [jnp.exp(v - vals[0]) for v in vals]
    den = es[0]
    for e in es[1:]:
        den = den + e
    sel = jnp.zeros(logits.shape, F32)
    for idx in idxs:
        sel = sel + jnp.where(lane_f == idx, 1.0, 0.0)

    @pl.when(i == 0)
    def _():
        carry_ref[...] = jnp.zeros_like(carry_ref)

    carry = carry_ref[...]
    r_io = lax.broadcasted_iota(I32, (TM, TM), 0)
    c_io = lax.broadcasted_iota(I32, (TM, TM), 1)
    tri = jnp.where(c_io < r_io, 1.0, 0.0).astype(BF16)
    rank = _dot(tri, sel.astype(BF16)) + carry
    carry = carry + sel.sum(axis=0, keepdims=True)
    carry_ref[...] = carry
    cnt_ref[...] = carry
    ri = jnp.zeros(logits.shape, F32)
    rg = jnp.zeros(logits.shape, F32)
    for k in range(TOP_K):
        rk = jnp.where(lane_f == idxs[k], rank, 0.0).sum(axis=-1, keepdims=True)
        ri = jnp.where(lane == k, idxs[k], ri)
        ri = jnp.where(lane == TOP_K + k, rk, ri)
        rg = jnp.where(lane == k, es[k] / den, rg)
    ri_ref[...] = ri.astype(I32)
    rg_ref[...] = rg


def _tail_specs(t):
    row = lambda w: pl.BlockSpec((TM, w), lambda i: (i, 0))
    out_specs = [row(D_MODEL), row(D_MODEL), row(LANES), row(LANES),
                 pl.BlockSpec((1, LANES), lambda i: (0, 0))]
    out_shape = [jax.ShapeDtypeStruct((t, D_MODEL), F32), jax.ShapeDtypeStruct((t, D_MODEL), F32),
                 jax.ShapeDtypeStruct((t, LANES), I32), jax.ShapeDtypeStruct((t, LANES), F32),
                 jax.ShapeDtypeStruct((1, LANES), F32)]
    return out_specs, out_shape


def _post_attn_kernel(x_ref, o_ref, mod_ref, w_ref, g_ref, wr_ref, br_ref,
                      xo_ref, h_ref, ri_ref, rg_ref, cnt_ref, carry_ref):
    i = pl.program_id(0)
    y = _dot(o_ref[...], w_ref[...])
    _router_tail(i, x_ref[...], y, mod_ref[0], g_ref[...], wr_ref[...], br_ref[...],
                 xo_ref, h_ref, ri_ref, rg_ref, cnt_ref, carry_ref)


def _post_attn_call(x, o, mod3, layer, w_out_bf, g_ffn, wr, br, npt, tps):
    t = x.shape[0]
    row = lambda w: pl.BlockSpec((TM, w), lambda i: (i, 0))
    full = lambda a: pl.BlockSpec(a.shape, lambda i: (0,) * a.ndim)
    out_specs, out_shape = _tail_specs(t)
    return pl.pallas_call(
        _post_attn_kernel,
        grid=(t // TM,),
        in_specs=[row(D_MODEL), row(D_MODEL),
                  pl.BlockSpec((1, 1, N_MOD * D_MODEL), lambda i: (layer * 3 + _cond_row(i, npt, tps), 0, 0)),
                  full(w_out_bf), full(g_ffn), full(wr), full(br)],
        out_specs=out_specs, out_shape=out_shape,
        scratch_shapes=[pltpu.VMEM((1, LANES), F32)],
        compiler_params=_cparams(1),
        name="attn_out_router",
    )(x, o, mod3, w_out_bf, g_ffn, wr, br)


def _pool_kernel(npt, tps, x_ref, xp_ref, xn_ref, mod_ref, gm_ref, wp_ref, ps_ref, g_ref, wr_ref, br_ref,
                 xo_ref, h_ref, ri_ref, rg_ref, cnt_ref, carry_ref):
    i = pl.program_id(0)
    mod = mod_ref[0]
    shift = mod[:, 0:D_MODEL]
    scale = mod[:, D_MODEL:2 * D_MODEL]
    gm = gm_ref[...]
    x = x_ref[...]
    pre = lambda v: _rms(v, gm) * (1.0 + scale) + shift
    j = jnp.where(i < npt, 0, (i - npt) % tps)
    ntile = jnp.where(i < npt, 1, tps)
    has_prev = (j > 0).astype(F32)
    has_next = (j < ntile - 1).astype(F32)
    h = pre(x)
    hc = jnp.concatenate([pre(xp_ref[...]) * has_prev, h, pre(xn_ref[...]) * has_next], axis=0)
    rows = hc.shape[0]
    t_seq = (j * TM + lax.broadcasted_iota(I32, (TM, 1), 0)).astype(F32)
    seq_len = (ntile * TM).astype(F32)
    ys = []
    for g, w in enumerate(POOL_WINDOWS):
        half = w // 2
        s = hc[:, g * POOL_CH:(g + 1) * POOL_CH]
        step = 1
        while step < w:
            s = s + pltpu.roll(s, rows - step, 0)
            step *= 2
        s = pltpu.roll(s, half, 0) if half != HALO else s
        win = s[HALO:HALO + TM] if half != HALO else s[0:TM]
        cnt = jnp.minimum(t_seq + half, seq_len) - jnp.maximum(t_seq - half, 0.0)
        pooled = win / cnt - h[:, g * POOL_CH:(g + 1) * POOL_CH]
        ys.append(_dot(pooled.astype(BF16), wp_ref[g]))
    y = jnp.concatenate(ys, axis=1) * ps_ref[...]
    _router_tail(i, x, y, mod, g_ref[...], wr_ref[...], br_ref[...],
                 xo_ref, h_ref, ri_ref, rg_ref, cnt_ref, carry_ref)


def _pool_call(x, mod3, layer, g_mix, w_pool_bf, pool_scale, g_ffn, wr, br, npt, tps):
    t = x.shape[0]
    per = TM // HALO
    nh = t // HALO
    row = lambda w: pl.BlockSpec((TM, w), lambda i: (i, 0))
    full = lambda a: pl.BlockSpec(a.shape, lambda i: (0,) * a.ndim)
    out_specs, out_shape = _tail_specs(t)
    return pl.pallas_call(
        functools.partial(_pool_kernel, npt, tps),
        grid=(t // TM,),
        in_specs=[row(D_MODEL),
                  pl.BlockSpec((HALO, D_MODEL), lambda i: (jnp.maximum(i * per - 1, 0), 0)),
                  pl.BlockSpec((HALO, D_MODEL), lambda i: (jnp.minimum((i + 1) * per, nh - 1), 0)),
                  pl.BlockSpec((1, 1, N_MOD * D_MODEL), lambda i: (layer * 3 + _cond_row(i, npt, tps), 0, 0)),
                  full(g_mix), full(w_pool_bf), full(pool_scale), full(g_ffn), full(wr), full(br)],
        out_specs=out_specs, out_shape=out_shape,
        scratch_shapes=[pltpu.VMEM((1, LANES), F32)],
        compiler_params=_cparams(1),
        name="pool_router",
    )(x, x, x, mod3, g_mix, w_pool_bf, pool_scale, g_ffn, wr, br)


def _expert_rows(x, e, wgu_bf, wd_bf, bgu_ref, bd_ref):
    bgu = bgu_ref[pl.ds(e, 1), :]
    acc = jnp.zeros((TM, D_MODEL), F32)
    for c in range(0, D_EXPERT, FF_CHUNK):
        gate = _dot(x, wgu_bf[:, c:c + FF_CHUNK]) + bgu[:, c:c + FF_CHUNK]
        up = _dot(x, wgu_bf[:, D_EXPERT + c:D_EXPERT + c + FF_CHUNK]) + bgu[:, D_EXPERT + c:D_EXPERT + c + FF_CHUNK]
        gate = jnp.minimum(gate, SWIGLU_LIMIT)
        up = jnp.clip(up, -SWIGLU_LIMIT, SWIGLU_LIMIT)
        act = gate * jax.nn.sigmoid(SWIGLU_ALPHA * gate) * (up + 1.0)
        acc = acc + _dot(act.astype(BF16), wd_bf[c:c + FF_CHUNK, :])
    return acc + bd_ref[pl.ds(e, 1), :]


def _gmm_kernel(layer, n_tok, pos_ref, starts_ref, nxt_ref, ef_ref, el_ref,
                h_hbm, wgu_hbm, wd_hbm, bgu_ref, bd_ref, yk_hbm,
                src_ref, cur_ref, xbuf, ybuf, wgu_st, wd_st, wgu_bf, wd_bf, sem_g, sem_s, sem_w):
    w = pl.program_id(0)
    n_tiles = pl.num_programs(0)
    slot = w % 2

    def weights_copy(e):
        return (pltpu.make_async_copy(wgu_hbm.at[layer, e], wgu_st, sem_w.at[0]),
                pltpu.make_async_copy(wd_hbm.at[layer, e], wd_st, sem_w.at[1]))

    def gather(tile, dst_slot, start):
        if start:
            def body(r, c):
                tok = lax.rem(src_ref[tile * TM + r], n_tok)
                pltpu.make_async_copy(h_hbm.at[pl.ds(tok, 1), :], xbuf.at[dst_slot, pl.ds(r, 1), :],
                                      sem_g.at[dst_slot]).start()
                return c
            lax.fori_loop(0, TM, body, 0, unroll=8)
        else:
            pltpu.make_async_copy(h_hbm.at[pl.ds(0, TM), :], xbuf.at[dst_slot], sem_g.at[dst_slot]).wait()

    def scatter(tile, src_slot, start):
        if start:
            def body(r, c):
                pltpu.make_async_copy(ybuf.at[src_slot, pl.ds(r, 1), :],
                                      yk_hbm.at[pl.ds(src_ref[tile * TM + r], 1), :],
                                      sem_s.at[src_slot]).start(priority=1)
                return c
            lax.fori_loop(0, TM, body, 0, unroll=8)
        else:
            pltpu.make_async_copy(ybuf.at[src_slot], yk_hbm.at[pl.ds(0, TM), :], sem_s.at[src_slot]).wait()

    @pl.when(w == 0)
    def _():
        def invert(a, c):
            src_ref[pos_ref[a]] = a
            return c
        lax.fori_loop(0, n_tok * TOP_K, invert, 0, unroll=8)
        ybuf[...] = jnp.zeros_like(ybuf)
        cur_ref[0] = -1
        for cp in weights_copy(ef_ref[0]):
            cp.start()
        gather(0, 0, True)

    gather(w, slot, False)

    @pl.when(w + 1 < n_tiles)
    def _():
        gather(w + 1, 1 - slot, True)

    @pl.when(w >= 2)
    def _():
        scatter(w - 2, slot, False)

    x = xbuf[slot].astype(BF16)
    row = lax.broadcasted_iota(I32, (TM, 1), 0)

    def one_expert(e, c):
        lo = starts_ref[e] - w * TM
        hi = starts_ref[e + 1] - w * TM

        @pl.when(hi > lo)
        def _():
            @pl.when(cur_ref[0] != e)
            def _():
                for cp in weights_copy(e):
                    cp.wait()
                wgu_bf[...] = wgu_st[...].astype(BF16)
                wd_bf[...] = wd_st[...].astype(BF16)
                cur_ref[0] = e

                @pl.when(nxt_ref[e] < N_EXPERTS)
                def _():
                    for cp in weights_copy(nxt_ref[e]):
                        cp.start()

            y = _expert_rows(x, e, wgu_bf, wd_bf, bgu_ref, bd_ref)
            ybuf[slot] = jnp.where((row >= lo) & (row < hi), y, ybuf[slot])
        return c

    lax.fori_loop(ef_ref[w], el_ref[w] + 1, one_expert, 0)
    scatter(w, slot, True)

    @pl.when(w == n_tiles - 1)
    def _():
        @pl.when(w >= 1)
        def _():
            scatter(w - 1, 1 - slot, False)
        scatter(w, slot, False)


def _gmm_call(pos, starts, nxt, ef, el, h, w_gu, w_dn, b_gu, b_dn, layer):
    t = h.shape[0]
    n_rows = t * TOP_K
    return pl.pallas_call(
        functools.partial(_gmm_kernel, layer, t),
        grid_spec=pltpu.PrefetchScalarGridSpec(
            num_scalar_prefetch=5, grid=(n_rows // TM,),
            in_specs=[pl.BlockSpec(memory_space=pl.ANY), pl.BlockSpec(memory_space=pl.ANY),
                      pl.BlockSpec(memory_space=pl.ANY),
                      pl.BlockSpec((None, N_EXPERTS, 2 * D_EXPERT), lambda w, *_: (layer, 0, 0)),
                      pl.BlockSpec((None, N_EXPERTS, D_MODEL), lambda w, *_: (layer, 0, 0))],
            out_specs=pl.BlockSpec(memory_space=pl.ANY),
            scratch_shapes=[
                pltpu.SMEM((n_rows,), I32), pltpu.SMEM((1,), I32),
                pltpu.VMEM((2, TM, D_MODEL), F32), pltpu.VMEM((2, TM, D_MODEL), F32),
                pltpu.VMEM((D_MODEL, 2 * D_EXPERT), F32), pltpu.VMEM((D_EXPERT, D_MODEL), F32),
                pltpu.VMEM((D_MODEL, 2 * D_EXPERT), BF16), pltpu.VMEM((D_EXPERT, D_MODEL), BF16),
                pltpu.SemaphoreType.DMA((2,)), pltpu.SemaphoreType.DMA((2,)), pltpu.SemaphoreType.DMA((2,))]),
        out_shape=jax.ShapeDtypeStruct((n_rows, D_MODEL), F32),
        compiler_params=_cparams(1),
        name="moe_experts",
    )(pos, starts, nxt, ef, el, h, w_gu, w_dn, b_gu, b_dn)


def _combine_kernel(final, x_ref, rg_ref, mod_ref, gf_ref, y0_ref, y1_ref, y2_ref, y3_ref, xo_ref, *rest):
    rg = rg_ref[...]
    moe = rg[:, 0:1] * y0_ref[...]
    for k, y_ref in enumerate((y1_ref, y2_ref, y3_ref), start=1):
        moe = moe + rg[:, k:k + 1] * y_ref[...]
    gate_ffn = mod_ref[0][:, 5 * D_MODEL:6 * D_MODEL]
    xn = x_ref[...] + gate_ffn * moe
    xo_ref[...] = xn
    if final:
        rest[0][...] = _rms(xn, gf_ref[...])


def _combine_call(yk, x, rg, mod3, layer, g_final, final, npt, tps):
    t = x.shape[0]
    nt = t // TM
    assert TOP_K == 4
    row = lambda w: pl.BlockSpec((TM, w), lambda i: (i, 0))
    slab = lambda k: pl.BlockSpec((TM, D_MODEL), lambda i: (k * nt + i, 0))
    n_out = 2 if final else 1
    return pl.pallas_call(
        functools.partial(_combine_kernel, final),
        grid=(nt,),
        in_specs=[row(D_MODEL), row(LANES),
                  pl.BlockSpec((1, 1, N_MOD * D_MODEL), lambda i: (layer * 3 + _cond_row(i, npt, tps), 0, 0)),
                  pl.BlockSpec(g_final.shape, lambda i: (0, 0))] + [slab(k) for k in range(TOP_K)],
        out_specs=[row(D_MODEL)] * n_out,
        out_shape=[jax.ShapeDtypeStruct((t, D_MODEL), F32)] * n_out,
        compiler_params=_cparams(1),
        name="moe_combine",
    )(x, rg, mod3, g_final, yk, yk, yk, yk)


def _routing_tables(ri, cnt, n_tok):
    ex = jnp.arange(N_EXPERTS, dtype=I32)
    counts = cnt[0, :N_EXPERTS].astype(I32)
    ends = jnp.sum(jnp.where(ex[None, :] <= ex[:, None], counts[None, :], 0), axis=1)
    starts = ends - counts
    e_idx = ri[:, :TOP_K]
    rank = ri[:, TOP_K:2 * TOP_K]
    onehot = e_idx[:, :, None] == ex[None, None, :]
    pos = rank + jnp.sum(jnp.where(onehot, starts[None, None, :], 0), axis=-1)
    pos = pos.T.reshape(-1).astype(I32)
    n_tiles = n_tok * TOP_K // TM
    row0 = jnp.arange(n_tiles, dtype=I32) * TM
    ef = jnp.sum((ends[None, :] <= row0[:, None]).astype(I32), axis=1)
    el = jnp.sum((ends[None, :] <= row0[:, None] + (TM - 1)).astype(I32), axis=1)
    later = (ex[None, :] > ex[:, None]) & (counts[None, :] > 0)
    nxt = jnp.min(jnp.where(later, ex[None, :], N_EXPERTS), axis=1)
    starts33 = jnp.concatenate([starts, ends[-1:]])
    return pos, starts33.astype(I32), nxt.astype(I32), ef.astype(I32), el.astype(I32)


def _moe(x, h, ri, rg, cnt, mod3, layer, w_gu, b_gu, w_dn, b_dn, g_final, final, npt, tps):
    pos, starts, nxt, ef, el = _routing_tables(ri, cnt, x.shape[0])
    yk = _gmm_call(pos, starts, nxt, ef, el, h, w_gu, w_dn, b_gu, b_dn, layer)
    return _combine_call(yk, x, rg, mod3, layer, g_final, final, npt, tps)


def _rope_tables(n_tokens):
    n_rows = n_tokens // GRID_W
    rows = jnp.repeat(jnp.arange(n_rows, dtype=F32), GRID_W)
    cols = jnp.tile(jnp.arange(GRID_W, dtype=F32), n_rows)
    axis_dim = A_HEAD_DIM // 2
    inv = ROPE_THETA ** (-jnp.arange(0, axis_dim, 2, dtype=F32) / axis_dim)
    ang_r = rows[:, None] * inv[None, :]
    ang_c = cols[:, None] * inv[None, :]
    ang = jnp.concatenate([ang_r, ang_r, ang_c, ang_c], axis=-1)
    cos, sin = jnp.cos(ang), jnp.sin(ang)
    sign = jnp.where((jnp.arange(A_HEAD_DIM) % 32) < 16, -1.0, 1.0).astype(F32)
    sin = sin * sign[None, :]
    cos = jnp.concatenate([jnp.ones((TM, A_HEAD_DIM), F32), cos], axis=0)
    sin = jnp.concatenate([jnp.zeros((TM, A_HEAD_DIM), F32), sin], axis=0)
    return jnp.tile(cos, (1, LANES // A_HEAD_DIM)), jnp.tile(sin, (1, LANES // A_HEAD_DIM))


def kernel(x_prompt, x_sample, cache_diff_k, cache_diff_v, cache_gqa_k, cache_gqa_v, c, c_ctx, w_ada, b_ada, norm_mix, norm_ffn, norm_final, w_attn_in, w_attn_out, lam_q1, lam_k1, lam_q2, lam_k2, diff_subln, gqa_q_norm, gqa_k_norm, w_pool, pool_scale, w_router, b_router, w_gate_up, b_gate_up, w_down, b_down):
    n_prompt, prompt_len, d = x_prompt.shape
    n_sample, sample_len, _ = x_sample.shape
    assert d == D_MODEL and prompt_len % TM == 0 and sample_len % TM == 0
    assert n_sample + 1 <= SUBLANES
    npt = n_prompt * prompt_len // TM
    tps = sample_len // TM
    tp = n_prompt * prompt_len
    past = cache_diff_k.shape[2]

    x = jnp.concatenate([x_prompt.reshape(tp, d), x_sample.reshape(n_sample * sample_len, d)], axis=0)
    cond8 = jnp.zeros((SUBLANES, d), F32).at[0].set(c_ctx).at[1:1 + n_sample].set(c)
    mod = _modulation_all(cond8, w_ada, b_ada)
    mod3 = mod[:, :1 + n_sample].reshape(DEPTH * (1 + n_sample), 1, N_MOD * d)
    assert n_sample == 2

    cos_t, sin_t = _rope_tables(sample_len)
    seg_r = jnp.arange(BQ_W)[:, None] // B_HEAD_DIM
    seg = (seg_r == seg_r.T).astype(BF16)

    caches = []
    y_final = None
    for i in range(DEPTH):
        j = i // 2
        g_mix = norm_mix[i][None]
        g_ffn = norm_ffn[i][None]
        wr = jnp.zeros((d, LANES), F32).at[:, :N_EXPERTS].set(w_router[i])
        br = jnp.full((1, LANES), NEG_BIG, F32).at[0, :N_EXPERTS].set(b_router[i])
        if i % 2 == 0:
            lambda_init = 0.8 - 0.6 * math.exp(-0.3 * i)
            gq = jnp.tile(gqa_q_norm[j], BQ_W // B_HEAD_DIM)[None]
            gk = jnp.tile(gqa_k_norm[j], BKV_W // B_HEAD_DIM)[None]
            qa, ka, va, qb, kbd, vb, ck, cv, cgk, cgv = _qkv_call(
                x, mod3, i, g_mix, w_attn_in[j].astype(BF16), gq, gk, cos_t, sin_t, seg, npt, tps)
            caches.append((ck, cv, cgk, cgv))
            cka = cache_diff_k[:, j].reshape(n_sample, past, A_W).astype(BF16)
            cva = cache_diff_v[:, j].reshape(n_sample, past, A_W).astype(BF16)
            gk_c = cache_gqa_k[:, j]
            ckbd = jnp.concatenate([gk_c[:, :, 0], gk_c[:, :, 0], gk_c[:, :, 1], gk_c[:, :, 1]],
                                   axis=-1).astype(BF16)
            cvb = cache_gqa_v[:, j].reshape(n_sample, past, BKV_W).astype(BF16)
            o = _attention(lambda_init, qa, ka, va, qb, kbd, vb, cka, cva, ckbd, cvb,
                           lam_q1[j][None], lam_k1[j][None], lam_q2[j][None], lam_k2[j][None],
                           diff_subln[j][None], n_prompt, prompt_len, n_sample, sample_len)
            x, h, ri, rg, cnt = _post_attn_call(x, o, mod3, i, w_attn_out[j].astype(BF16), g_ffn, wr, br, npt, tps)
        else:
            x, h, ri, rg, cnt = _pool_call(x, mod3, i, g_mix, w_pool[j].astype(BF16), pool_scale[j][None],
                                           g_ffn, wr, br, npt, tps)
        final = i == DEPTH - 1
        outs = _moe(x, h, ri, rg, cnt, mod3, i, w_gate_up, b_gate_up, w_down, b_down, norm_final[None], final, npt, tps)
        x = outs[0]
        if final:
            y_final = outs[1]

    y_prompt = y_final[:tp].reshape(n_prompt, prompt_len, d)
    y_sample = y_final[tp:].reshape(n_sample, sample_len, d)
    stack = lambda k, shp: jnp.stack([cc[k].reshape(shp) for cc in caches], axis=1)
    new_diff_k = stack(0, (n_prompt, prompt_len, A_HEADS, 2 * A_HEAD_DIM))
    new_diff_v = stack(1, (n_prompt, prompt_len, A_HEADS, 2 * A_HEAD_DIM))
    new_gqa_k = stack(2, (n_prompt, prompt_len, B_KV_HEADS, B_HEAD_DIM))
    new_gqa_v = stack(3, (n_prompt, prompt_len, B_KV_HEADS, B_HEAD_DIM))
    return (y_prompt, y_sample, new_diff_k, new_diff_v, new_gqa_k, new_gqa_v)
```

```python
import functools
import math

import jax
import jax.numpy as jnp
from jax import lax
from jax.experimental import pallas as pl
from jax.experimental.pallas import tpu as pltpu

F32 = jnp.float32
BF16 = jnp.bfloat16
I32 = jnp.int32

D_MODEL = 1024
DEPTH = 4
GRID_W = 64
A_HEADS = 4
A_HEAD_DIM = 64
B_Q_HEADS = 8
B_KV_HEADS = 2
B_HEAD_DIM = 64
ROPE_THETA = 10000.0
A_W = A_HEADS * 2 * A_HEAD_DIM
BQ_W = B_Q_HEADS * B_HEAD_DIM
BKV_W = B_KV_HEADS * B_HEAD_DIM
IN_W = 3 * A_W + BQ_W + 2 * BKV_W
POOL_WINDOWS = (2, 4, 8, 16)
POOL_GROUPS = 4
POOL_CH = D_MODEL // POOL_GROUPS
N_EXPERTS = 32
TOP_K = 4
D_EXPERT = D_MODEL
SWIGLU_LIMIT = 7.0
SWIGLU_ALPHA = 1.702
N_MOD = 6
EPS = 1e-6

LANES = 128
SUBLANES = 8
TM = 256
ROW_TILES = D_MODEL // LANES
HALO = 8
ADA_TN = 1536
FF_CHUNK = 512
VMEM_LIMIT = 56 * 1024 * 1024
NEG_BIG = -1e30


def _cparams(n_axes, vmem=VMEM_LIMIT):
    return pltpu.CompilerParams(
        dimension_semantics=("arbitrary",) * n_axes, vmem_limit_bytes=vmem)


def _dot(a, b):
    return jnp.dot(a, b, preferred_element_type=F32)


def _dot_nt(a, b):
    return lax.dot_general(a, b, (((1,), (1,)), ((), ())), preferred_element_type=F32)


def _rms(x, g):
    ms = jnp.mean(x * x, axis=-1, keepdims=True)
    return x * lax.rsqrt(ms + EPS) * g


def _lane_iota(shape):
    return lax.broadcasted_iota(I32, shape, len(shape) - 1)


def _store_token_tiles(ref, x, mask=None):
    n = x.shape[0]
    for c in range(ROW_TILES):
        idx = (pl.ds(c, n, stride=ROW_TILES), slice(None))
        v = x[:, c * LANES:(c + 1) * LANES]
        ref[idx] = v if mask is None else jnp.where(mask, v, ref[idx])


def _load_token_tiles(ref, n):
    return jnp.concatenate([ref[pl.ds(c, n, stride=ROW_TILES), :] for c in range(ROW_TILES)], axis=1)


def _ada_kernel(c_ref, w_ref, b_ref, o_ref):
    c = c_ref[...]
    s = (c * jax.nn.sigmoid(c)).astype(BF16)
    o_ref[0] = _dot(s, w_ref[0].astype(BF16)) + b_ref[0]


def _modulation_all(cond8, w_ada, b_ada):
    nmod = w_ada.shape[-1]
    return pl.pallas_call(
        _ada_kernel,
        grid=(DEPTH, nmod // ADA_TN),
        in_specs=[
            pl.BlockSpec((SUBLANES, D_MODEL), lambda l, n: (0, 0)),
            pl.BlockSpec((1, D_MODEL, ADA_TN), lambda l, n: (l, 0, n)),
            pl.BlockSpec((1, 1, ADA_TN), lambda l, n: (l, 0, n)),
        ],
        out_specs=pl.BlockSpec((1, SUBLANES, ADA_TN), lambda l, n: (l, 0, n)),
        out_shape=jax.ShapeDtypeStruct((DEPTH, SUBLANES, nmod), F32),
        compiler_params=_cparams(2),
        name="modulation",
    )(cond8, w_ada, b_ada.reshape(DEPTH, 1, nmod))


def _seg_meansq(x, seg):
    x2 = x * x
    hi = x2.astype(BF16)
    lo = (x2 - hi.astype(F32)).astype(BF16)
    return (_dot(hi, seg) + _dot(lo, seg)) * (1.0 / B_HEAD_DIM)


def _rope128(x, cos, sin_signed):
    lo16 = (_lane_iota(x.shape) % 32) < 16
    nxt = pltpu.roll(x, LANES - 16, 1)
    prv = pltpu.roll(x, 16, 1)
    return x * cos + jnp.where(lo16, nxt, prv) * sin_signed


def _rope(x, cos, sin_signed):
    cols = [_rope128(x[:, c:c + LANES], cos, sin_signed) for c in range(0, x.shape[1], LANES)]
    return cols[0] if len(cols) == 1 else jnp.concatenate(cols, axis=1)


def _qkv_kernel(n_prompt_tiles, x_ref, mod_ref, g_ref, w_ref, gq_ref, gk_ref, cos_ref, sin_ref,
                seg_ref, qa_ref, ka_ref, va_ref, qb_ref, kbd_ref, vb_ref,
                ck_ref, cv_ref, cgk_ref, cgv_ref):
    i = pl.program_id(0)
    mod = mod_ref[0]
    shift = mod[:, 0:D_MODEL]
    scale = mod[:, D_MODEL:2 * D_MODEL]
    h = _rms(x_ref[...], g_ref[...]) * (1.0 + scale) + shift
    p = _dot(h.astype(BF16), w_ref[...])
    a_q = p[:, 0:A_W]
    a_k = p[:, A_W:2 * A_W]
    a_v = p[:, 2 * A_W:3 * A_W]
    o = 3 * A_W
    b_q = p[:, o:o + BQ_W]
    b_k = p[:, o + BQ_W:o + BQ_W + BKV_W]
    b_v = p[:, o + BQ_W + BKV_W:o + BQ_W + 2 * BKV_W]
    seg = seg_ref[...]
    b_q = b_q * lax.rsqrt(_seg_meansq(b_q, seg) + EPS) * gq_ref[...]
    b_k = b_k * lax.rsqrt(_seg_meansq(b_k, seg[:BKV_W, :BKV_W]) + EPS) * gk_ref[...]

    @pl.when(i < n_prompt_tiles)
    def _():
        ck_ref[...] = a_k
        cv_ref[...] = a_v
        cgk_ref[...] = b_k
        cgv_ref[...] = b_v

    cos = cos_ref[...]
    sin = sin_ref[...]
    sm = A_HEAD_DIM ** -0.5
    qa_ref[...] = (_rope(a_q, cos, sin) * sm).astype(BF16)
    ka_ref[...] = _rope(a_k, cos, sin).astype(BF16)
    va_ref[...] = a_v.astype(BF16)
    qb_ref[...] = (_rope(b_q, cos, sin) * (B_HEAD_DIM ** -0.5)).astype(BF16)
    kb = _rope(b_k, cos, sin)
    kb_sw = pltpu.roll(kb, B_HEAD_DIM, 1)
    lo = _lane_iota(kb.shape) < B_HEAD_DIM
    kbd_ref[...] = jnp.concatenate(
        [jnp.where(lo, kb, kb_sw), jnp.where(lo, kb_sw, kb)], axis=1).astype(BF16)
    vb_ref[...] = b_v.astype(BF16)


def _cond_row(i, npt, tiles_per_seq):
    return jnp.where(i < npt, 0, 1 + (i - npt) // tiles_per_seq)


def _qkv_call(x, mod3, layer, g_mix, w_in_bf, gq, gk, cos_t, sin_t, seg, npt, tps):
    t = x.shape[0]
    nt = t // TM
    tp = npt * TM
    row = lambda w: pl.BlockSpec((TM, w), lambda i: (i, 0))
    full = lambda a: pl.BlockSpec(a.shape, lambda i: (0,) * a.ndim)
    tab = pl.BlockSpec((TM, LANES), lambda i: (jnp.where(i < npt, 0, 1 + (i - npt) % tps), 0))
    cache = lambda w: pl.BlockSpec((TM, w), lambda i: (jnp.minimum(i, npt - 1), 0))
    return pl.pallas_call(
        functools.partial(_qkv_kernel, npt),
        grid=(nt,),
        in_specs=[
            row(D_MODEL),
            pl.BlockSpec((1, 1, N_MOD * D_MODEL), lambda i: (layer * 3 + _cond_row(i, npt, tps), 0, 0)),
            full(g_mix), full(w_in_bf), full(gq), full(gk), tab, tab, full(seg),
        ],
        out_specs=[row(A_W), row(A_W), row(A_W), row(BQ_W), row(2 * BKV_W), row(BKV_W),
                   cache(A_W), cache(A_W), cache(BKV_W), cache(BKV_W)],
        out_shape=[
            jax.ShapeDtypeStruct((t, A_W), BF16), jax.ShapeDtypeStruct((t, A_W), BF16),
            jax.ShapeDtypeStruct((t, A_W), BF16), jax.ShapeDtypeStruct((t, BQ_W), BF16),
            jax.ShapeDtypeStruct((t, 2 * BKV_W), BF16), jax.ShapeDtypeStruct((t, BKV_W), BF16),
            jax.ShapeDtypeStruct((tp, A_W), F32), jax.ShapeDtypeStruct((tp, A_W), F32),
            jax.ShapeDtypeStruct((tp, BKV_W), F32), jax.ShapeDtypeStruct((tp, BKV_W), F32),
        ],
        compiler_params=_cparams(1),
        name="attn_qkv",
    )(x, mod3, g_mix, w_in_bf, gq, gk, cos_t, sin_t, seg)


def _softmax_pv(qq, ks, vs):
    ss = [_dot_nt(qq, k) for k in ks]
    m = ss[0].max(axis=-1, keepdims=True)
    for s in ss[1:]:
        m = jnp.maximum(m, s.max(axis=-1, keepdims=True))
    acc = None
    l = None
    for s, v in zip(ss, vs):
        p = jnp.exp(s - m)
        ls = p.sum(axis=-1, keepdims=True)
        pv = _dot(p.astype(BF16), v)
        acc = pv if acc is None else acc + pv
        l = ls if l is None else l + ls
    return acc / l


def _split_halves(q):
    lo = _lane_iota(q.shape) < (LANES // 2)
    zero = jnp.zeros_like(q)
    return jnp.concatenate([jnp.where(lo, q, zero), jnp.where(lo, zero, q)], axis=0)


def _attn_body(lambda_init, qa, qb, ka_segs, va_segs, kbd_segs, vb_segs, lam, subln):
    tq = qa.shape[0]
    cols = []
    for h in range(A_HEADS):
        sl = slice(h * LANES, (h + 1) * LANES)
        o = _softmax_pv(_split_halves(qa[:, sl]), [k[:, sl] for k in ka_segs],
                        [v[:, sl] for v in va_segs])
        od = o[:tq] - lam * o[tq:]
        cols.append(_rms(od, subln) * (1.0 - lambda_init))
    lo = _lane_iota((tq, LANES)) < B_HEAD_DIM
    for c in range(BQ_W // LANES):
        g = (2 * c) // (B_Q_HEADS // B_KV_HEADS)
        sl = slice(c * LANES, (c + 1) * LANES)
        gl = slice(g * LANES, (g + 1) * LANES)
        o = _softmax_pv(_split_halves(qb[:, sl]), [k[:, gl] for k in kbd_segs], vb_segs)
        oe, oo = o[:tq], o[tq:]
        if g == 0:
            cols.append(jnp.where(lo, oe, pltpu.roll(oo, B_HEAD_DIM, 1)))
        else:
            cols.append(jnp.where(lo, pltpu.roll(oe, B_HEAD_DIM, 1), oo))
    return jnp.concatenate(cols, axis=1).astype(BF16)


def _lambda(lq1, lk1, lq2, lk2, lambda_init):
    return (jnp.exp(jnp.sum(lq1 * lk1, axis=-1, keepdims=True))
            - jnp.exp(jnp.sum(lq2 * lk2, axis=-1, keepdims=True)) + lambda_init)


def _attn_prompt_kernel(lambda_init, qa, ka, va, qb, kbd, vb, lq1, lk1, lq2, lk2, subln, o_ref):
    lam = _lambda(lq1[...], lk1[...], lq2[...], lk2[...], lambda_init)
    o_ref[...] = _attn_body(lambda_init, qa[...], qb[...], [ka[...]], [va[...]],
                            [kbd[...]], [vb[...]], lam, subln[...])


def _attn_sample_kernel(lambda_init, qa, ka, va, qb, kbd, vb, cka, cva, ckbd, cvb,
                        lq1, lk1, lq2, lk2, subln, o_ref):
    lam = _lambda(lq1[...], lk1[...], lq2[...], lk2[...], lambda_init)
    o_ref[...] = _attn_body(lambda_init, qa[...], qb[...], [ka[...], cka[...]], [va[...], cva[...]],
                            [kbd[...], ckbd[...]], [vb[...], cvb[...]], lam, subln[...])


def _attention(lambda_init, qa, ka, va, qb, kbd, vb, cka, cva, ckbd, cvb,
               lq1, lk1, lq2, lk2, subln, n_prompt, prompt_len, n_sample, sample_len):
    npt = n_prompt * prompt_len // TM
    small = [lq1, lk1, lq2, lk2, subln]
    widths = [A_W, A_W, A_W, BQ_W, 2 * BKV_W, BKV_W]
    assert prompt_len == TM
    full1 = lambda a: pl.BlockSpec(a.shape, lambda b: (0,) * a.ndim)
    o_p = pl.pallas_call(
        functools.partial(_attn_prompt_kernel, lambda_init),
        grid=(n_prompt,),
        in_specs=[pl.BlockSpec((TM, w), lambda b: (b, 0)) for w in widths] + [full1(a) for a in small],
        out_specs=pl.BlockSpec((TM, D_MODEL), lambda b: (b, 0)),
        out_shape=jax.ShapeDtypeStruct((npt * TM, D_MODEL), BF16),
        compiler_params=_cparams(1),
        name="attn_prompt",
    )(qa, ka, va, qb, kbd, vb, *small)
    tps = sample_len // TM
    seq0 = n_prompt * prompt_len // sample_len
    assert seq0 * sample_len == n_prompt * prompt_len
    past = cka.shape[1]
    qspec = lambda w: pl.BlockSpec((TM, w), lambda b, q: (npt + b * tps + q, 0))
    kspec = lambda w: pl.BlockSpec((sample_len, w), lambda b, q: (seq0 + b, 0))
    cspec = lambda w: pl.BlockSpec((None, past, w), lambda b, q: (b, 0, 0))
    full2 = lambda a: pl.BlockSpec(a.shape, lambda b, q: (0,) * a.ndim)
    o_s = pl.pallas_call(
        functools.partial(_attn_sample_kernel, lambda_init),
        grid=(n_sample, tps),
        in_specs=[qspec(A_W), kspec(A_W), kspec(A_W), qspec(BQ_W), kspec(2 * BKV_W), kspec(BKV_W),
                  cspec(A_W), cspec(A_W), cspec(2 * BKV_W), cspec(BKV_W)] + [full2(a) for a in small],
        out_specs=pl.BlockSpec((TM, D_MODEL), lambda b, q: (b * tps + q, 0)),
        out_shape=jax.ShapeDtypeStruct((n_sample * sample_len, D_MODEL), BF16),
        compiler_params=_cparams(2),
        name="attn_sample",
    )(qa, ka, va, qb, kbd, vb, cka, cva, ckbd, cvb, *small)
    return jnp.concatenate([o_p, o_s], axis=0)


def _router_tail(i, x, y, mod, g_ffn, wr, br, xo_ref, h_ref, ri_ref, rg_ref, cnt_ref, carry_ref):
    gate_mix = mod[:, 2 * D_MODEL:3 * D_MODEL]
    shift = mod[:, 3 * D_MODEL:4 * D_MODEL]
    scale = mod[:, 4 * D_MODEL:5 * D_MODEL]
    xn = x + gate_mix * y
    xo_ref[...] = xn
    h = _rms(xn, g_ffn) * (1.0 + scale) + shift
    _store_token_tiles(h_ref, h)
    h_hi = h.astype(BF16)
    h_lo = (h - h_hi.astype(F32)).astype(BF16)
    w_hi = wr.astype(BF16)
    w_lo = (wr - w_hi.astype(F32)).astype(BF16)
    logits = _dot(h_hi, w_hi) + _dot(h_lo, w_hi) + _dot(h_hi, w_lo) + br
    lane = _lane_iota(logits.shape)
    lane_f = lane.astype(F32)
    vals, idxs = [], []
    l = logits
    for _ in range(TOP_K):
        m = l.max(axis=-1, keepdims=True)
        idx = jnp.where(l == m, lane_f, float(LANES)).min(axis=-1, keepdims=True)
        vals.append(m)
        idxs.append(idx)
        l = jnp.where(lane_f == idx, -jnp.inf, l)
    es = [jnp.exp(v - vals[0]) for v in vals]
    den = es[0]
    for e in es[1:]:
        den = den + e
    sel = jnp.zeros(logits.shape, F32)
    for idx in idxs:
        sel = sel + jnp.where(lane_f == idx, 1.0, 0.0)

    @pl.when(i == 0)
    def _():
        carry_ref[...] = jnp.zeros_like(carry_ref)

    carry = carry_ref[...]
    r_io = lax.broadcasted_iota(I32, (TM, TM), 0)
    c_io = lax.broadcasted_iota(I32, (TM, TM), 1)
    tri = jnp.where(c_io < r_io, 1.0, 0.0).astype(BF16)
    rank = _dot(tri, sel.astype(BF16)) + carry
    carry = carry + sel.sum(axis=0, keepdims=True)
    carry_ref[...] = carry
    cnt_ref[...] = carry
    ri = jnp.zeros(logits.shape, F32)
    rg = jnp.zeros(logits.shape, F32)
    for k in range(TOP_K):
        rk = jnp.where(lane_f == idxs[k], rank, 0.0).sum(axis=-1, keepdims=True)
        ri = jnp.where(lane == k, idxs[k], ri)
        ri = jnp.where(lane == TOP_K + k, rk, ri)
        rg = jnp.where(lane == k, es[k] / den, rg)
    ri_ref[...] = ri.astype(I32)
    rg_ref[...] = rg


def _tail_specs(t):
    row = lambda w: pl.BlockSpec((TM, w), lambda i: (i, 0))
    out_specs = [row(D_MODEL), pl.BlockSpec((TM * ROW_TILES, LANES), lambda i: (i, 0)), row(LANES), row(LANES),
                 pl.BlockSpec((1, LANES), lambda i: (0, 0))]
    out_shape = [jax.ShapeDtypeStruct((t, D_MODEL), F32), jax.ShapeDtypeStruct((t * ROW_TILES, LANES), F32),
                 jax.ShapeDtypeStruct((t, LANES), I32), jax.ShapeDtypeStruct((t, LANES), F32),
                 jax.ShapeDtypeStruct((1, LANES), F32)]
    return out_specs, out_shape


def _post_attn_kernel(x_ref, o_ref, mod_ref, w_ref, g_ref, wr_ref, br_ref,
                      xo_ref, h_ref, ri_ref, rg_ref, cnt_ref, carry_ref):
    i = pl.program_id(0)
    y = _dot(o_ref[...], w_ref[...])
    _router_tail(i, x_ref[...], y, mod_ref[0], g_ref[...], wr_ref[...], br_ref[...],
                 xo_ref, h_ref, ri_ref, rg_ref, cnt_ref, carry_ref)


def _post_attn_call(x, o, mod3, layer, w_out_bf, g_ffn, wr, br, npt, tps):
    t = x.shape[0]
    row = lambda w: pl.BlockSpec((TM, w), lambda i: (i, 0))
    full = lambda a: pl.BlockSpec(a.shape, lambda i: (0,) * a.ndim)
    out_specs, out_shape = _tail_specs(t)
    return pl.pallas_call(
        _post_attn_kernel,
        grid=(t // TM,),
        in_specs=[row(D_MODEL), row(D_MODEL),
                  pl.BlockSpec((1, 1, N_MOD * D_MODEL), lambda i: (layer * 3 + _cond_row(i, npt, tps), 0, 0)),
                  full(w_out_bf), full(g_ffn), full(wr), full(br)],
        out_specs=out_specs, out_shape=out_shape,
        scratch_shapes=[pltpu.VMEM((1, LANES), F32)],
        compiler_params=_cparams(1),
        name="attn_out_router",
    )(x, o, mod3, w_out_bf, g_ffn, wr, br)


def _pool_kernel(npt, tps, x_ref, xp_ref, xn_ref, mod_ref, gm_ref, wp_ref, ps_ref, g_ref, wr_ref, br_ref,
                 xo_ref, h_ref, ri_ref, rg_ref, cnt_ref, carry_ref):
    i = pl.program_id(0)
    mod = mod_ref[0]
    shift = mod[:, 0:D_MODEL]
    scale = mod[:, D_MODEL:2 * D_MODEL]
    gm = gm_ref[...]
    x = x_ref[...]
    pre = lambda v: _rms(v, gm) * (1.0 + scale) + shift
    j = jnp.where(i < npt, 0, (i - npt) % tps)
    ntile = jnp.where(i < npt, 1, tps)
    has_prev = (j > 0).astype(F32)
    has_next = (j < ntile - 1).astype(F32)
    h = pre(x)
    hc = jnp.concatenate([pre(xp_ref[...]) * has_prev, h, pre(xn_ref[...]) * has_next], axis=0)
    rows = hc.shape[0]
    t_seq = (j * TM + lax.broadcasted_iota(I32, (TM, 1), 0)).astype(F32)
    seq_len = (ntile * TM).astype(F32)
    ys = []
    for g, w in enumerate(POOL_WINDOWS):
        half = w // 2
        s = hc[:, g * POOL_CH:(g + 1) * POOL_CH]
        step = 1
        while step < w:
            s = s + pltpu.roll(s, rows - step, 0)
            step *= 2
        s = pltpu.roll(s, half, 0) if half != HALO else s
        win = s[HALO:HALO + TM] if half != HALO else s[0:TM]
        cnt = jnp.minimum(t_seq + half, seq_len) - jnp.maximum(t_seq - half, 0.0)
        pooled = win / cnt - h[:, g * POOL_CH:(g + 1) * POOL_CH]
        ys.append(_dot(pooled.astype(BF16), wp_ref[g]))
    y = jnp.concatenate(ys, axis=1) * ps_ref[...]
    _router_tail(i, x, y, mod, g_ref[...], wr_ref[...], br_ref[...],
                 xo_ref, h_ref, ri_ref, rg_ref, cnt_ref, carry_ref)


def _pool_call(x, mod3, layer, g_mix, w_pool_bf, pool_scale, g_ffn, wr, br, npt, tps):
    t = x.shape[0]
    per = TM // HALO
    nh = t // HALO
    row = lambda w: pl.BlockSpec((TM, w), lambda i: (i, 0))
    full = lambda a: pl.BlockSpec(a.shape, lambda i: (0,) * a.ndim)
    out_specs, out_shape = _tail_specs(t)
    return pl.pallas_call(
        functools.partial(_pool_kernel, npt, tps),
        grid=(t // TM,),
        in_specs=[row(D_MODEL),
                  pl.BlockSpec((HALO, D_MODEL), lambda i: (jnp.maximum(i * per - 1, 0), 0)),
                  pl.BlockSpec((HALO, D_MODEL), lambda i: (jnp.minimum((i + 1) * per, nh - 1), 0)),
                  pl.BlockSpec((1, 1, N_MOD * D_MODEL), lambda i: (layer * 3 + _cond_row(i, npt, tps), 0, 0)),
                  full(g_mix), full(w_pool_bf), full(pool_scale), full(g_ffn), full(wr), full(br)],
        out_specs=out_specs, out_shape=out_shape,
        scratch_shapes=[pltpu.VMEM((1, LANES), F32)],
        compiler_params=_cparams(1),
        name="pool_router",
    )(x, x, x, mod3, g_mix, w_pool_bf, pool_scale, g_ffn, wr, br)


def _expert_rows(x, e, wgu_bf, wd_bf, bgu_ref, bd_ref):
    bgu = bgu_ref[pl.ds(e, 1), :]
    acc = jnp.zeros((TM, D_MODEL), F32)
    for c in range(0, D_EXPERT, FF_CHUNK):
        gate = _dot(x, wgu_bf[:, c:c + FF_CHUNK]) + bgu[:, c:c + FF_CHUNK]
        up = _dot(x, wgu_bf[:, D_EXPERT + c:D_EXPERT + c + FF_CHUNK]) + bgu[:, D_EXPERT + c:D_EXPERT + c + FF_CHUNK]
        gate = jnp.minimum(gate, SWIGLU_LIMIT)
        up = jnp.clip(up, -SWIGLU_LIMIT, SWIGLU_LIMIT)
        act = gate * jax.nn.sigmoid(SWIGLU_ALPHA * gate) * (up + 1.0)
        acc = acc + _dot(act.astype(BF16), wd_bf[c:c + FF_CHUNK, :])
    return acc + bd_ref[pl.ds(e, 1), :]


def _gmm_kernel(layer, n_tok, pos_ref, starts_ref, nxt_ref, ef_ref, el_ref,
                h_hbm, wgu_hbm, wd_hbm, bgu_ref, bd_ref, yk_hbm,
                src_ref, cur_ref, xbuf, ybuf, wgu_st, wd_st, wgu_bf, wd_bf, sem_g, sem_s, sem_w):
    w = pl.program_id(0)
    n_tiles = pl.num_programs(0)
    slot = w % 2
    blk = TM * ROW_TILES

    def weights_copy(e):
        return (pltpu.make_async_copy(wgu_hbm.at[layer, e], wgu_st, sem_w.at[0]),
                pltpu.make_async_copy(wd_hbm.at[layer, e], wd_st, sem_w.at[1]))

    def tile_of(ref, start):
        return ref.at[pl.ds(pl.multiple_of(start, ROW_TILES), ROW_TILES), :]

    def gather(tile, dst_slot, start):
        if start:
            def body(r, c):
                tok_row = src_ref[tile * TM + r] & (n_tok * ROW_TILES - 1)
                pltpu.make_async_copy(tile_of(h_hbm, tok_row),
                                      tile_of(xbuf.at[dst_slot], r * ROW_TILES), sem_g.at[dst_slot]).start()
                return c
            lax.fori_loop(0, TM, body, 0, unroll=8)
        else:
            pltpu.make_async_copy(h_hbm.at[pl.ds(0, blk), :], xbuf.at[dst_slot], sem_g.at[dst_slot]).wait()

    def scatter(tile, src_slot, start):
        if start:
            def body(r, c):
                pltpu.make_async_copy(tile_of(ybuf.at[src_slot], r * ROW_TILES),
                                      tile_of(yk_hbm, src_ref[tile * TM + r]),
                                      sem_s.at[src_slot]).start(priority=1)
                return c
            lax.fori_loop(0, TM, body, 0, unroll=8)
        else:
            pltpu.make_async_copy(ybuf.at[src_slot], yk_hbm.at[pl.ds(0, blk), :], sem_s.at[src_slot]).wait()

    @pl.when(w == 0)
    def _():
        def invert(a, c):
            src_ref[pos_ref[a]] = a * ROW_TILES
            return c
        lax.fori_loop(0, n_tok * TOP_K, invert, 0, unroll=8)
        ybuf[...] = jnp.zeros_like(ybuf)
        cur_ref[0] = -1
        for cp in weights_copy(ef_ref[0]):
            cp.start()
        gather(0, 0, True)

    gather(w, slot, False)

    @pl.when(w + 1 < n_tiles)
    def _():
        gather(w + 1, 1 - slot, True)

    @pl.when(w >= 2)
    def _():
        scatter(w - 2, slot, False)

    x = _load_token_tiles(xbuf.at[slot], TM).astype(BF16)
    row = lax.broadcasted_iota(I32, (TM, 1), 0)

    def one_expert(e, c):
        lo = starts_ref[e] - w * TM
        hi = starts_ref[e + 1] - w * TM

        @pl.when(hi > lo)
        def _():
            @pl.when(cur_ref[0] != e)
            def _():
                for cp in weights_copy(e):
                    cp.wait()
                wgu_bf[...] = wgu_st[...].astype(BF16)
                wd_bf[...] = wd_st[...].astype(BF16)
                cur_ref[0] = e

                @pl.when(nxt_ref[e] < N_EXPERTS)
                def _():
                    for cp in weights_copy(nxt_ref[e]):
                        cp.start()

            y = _expert_rows(x, e, wgu_bf, wd_bf, bgu_ref, bd_ref)
            _store_token_tiles(ybuf.at[slot], y, mask=(row >= lo) & (row < hi))
        return c

    lax.fori_loop(ef_ref[w], el_ref[w] + 1, one_expert, 0)
    scatter(w, slot, True)

    @pl.when(w == n_tiles - 1)
    def _():
        @pl.when(w >= 1)
        def _():
            scatter(w - 1, 1 - slot, False)
        scatter(w, slot, False)


def _gmm_call(pos, starts, nxt, ef, el, h, w_gu, w_dn, b_gu, b_dn, layer):
    assert ROW_TILES == SUBLANES
    t = h.shape[0] // ROW_TILES
    assert t & (t - 1) == 0
    n_rows = t * TOP_K
    return pl.pallas_call(
        functools.partial(_gmm_kernel, layer, t),
        grid_spec=pltpu.PrefetchScalarGridSpec(
            num_scalar_prefetch=5, grid=(n_rows // TM,),
            in_specs=[pl.BlockSpec(memory_space=pl.ANY), pl.BlockSpec(memory_space=pl.ANY),
                      pl.BlockSpec(memory_space=pl.ANY),
                      pl.BlockSpec((None, N_EXPERTS, 2 * D_EXPERT), lambda w, *_: (layer, 0, 0)),
                      pl.BlockSpec((None, N_EXPERTS, D_MODEL), lambda w, *_: (layer, 0, 0))],
            out_specs=pl.BlockSpec(memory_space=pl.ANY),
            scratch_shapes=[
                pltpu.SMEM((n_rows,), I32), pltpu.SMEM((1,), I32),
                pltpu.VMEM((2, TM * ROW_TILES, LANES), F32), pltpu.VMEM((2, TM * ROW_TILES, LANES), F32),
                pltpu.VMEM((D_MODEL, 2 * D_EXPERT), F32), pltpu.VMEM((D_EXPERT, D_MODEL), F32),
                pltpu.VMEM((D_MODEL, 2 * D_EXPERT), BF16), pltpu.VMEM((D_EXPERT, D_MODEL), BF16),
                pltpu.SemaphoreType.DMA((2,)), pltpu.SemaphoreType.DMA((2,)), pltpu.SemaphoreType.DMA((2,))]),
        out_shape=jax.ShapeDtypeStruct((n_rows * ROW_TILES, LANES), F32),
        compiler_params=_cparams(1),
        name="moe_experts",
    )(pos, starts, nxt, ef, el, h, w_gu, w_dn, b_gu, b_dn)


def _combine_kernel(final, x_ref, rg_ref, mod_ref, gf_ref, y0_ref, y1_ref, y2_ref, y3_ref, xo_ref, *rest):
    rg = rg_ref[...]
    moe = rg[:, 0:1] * _load_token_tiles(y0_ref, TM)
    for k, y_ref in enumerate((y1_ref, y2_ref, y3_ref), start=1):
        moe = moe + rg[:, k:k + 1] * _load_token_tiles(y_ref, TM)
    gate_ffn = mod_ref[0][:, 5 * D_MODEL:6 * D_MODEL]
    xn = x_ref[...] + gate_ffn * moe
    xo_ref[...] = xn
    if final:
        rest[0][...] = _rms(xn, gf_ref[...])


def _combine_call(yk, x, rg, mod3, layer, g_final, final, npt, tps):
    t = x.shape[0]
    nt = t // TM
    assert TOP_K == 4
    row = lambda w: pl.BlockSpec((TM, w), lambda i: (i, 0))
    slab = lambda k: pl.BlockSpec((TM * ROW_TILES, LANES), lambda i: (k * nt + i, 0))
    n_out = 2 if final else 1
    return pl.pallas_call(
        functools.partial(_combine_kernel, final),
        grid=(nt,),
        in_specs=[row(D_MODEL), row(LANES),
                  pl.BlockSpec((1, 1, N_MOD * D_MODEL), lambda i: (layer * 3 + _cond_row(i, npt, tps), 0, 0)),
                  pl.BlockSpec(g_final.shape, lambda i: (0, 0))] + [slab(k) for k in range(TOP_K)],
        out_specs=[row(D_MODEL)] * n_out,
        out_shape=[jax.ShapeDtypeStruct((t, D_MODEL), F32)] * n_out,
        compiler_params=_cparams(1),
        name="moe_combine",
    )(x, rg, mod3, g_final, yk, yk, yk, yk)


def _routing_tables(ri, cnt, n_tok):
    ex = jnp.arange(N_EXPERTS, dtype=I32)
    counts = cnt[0, :N_EXPERTS].astype(I32)
    ends = jnp.sum(jnp.where(ex[None, :] <= ex[:, None], counts[None, :], 0), axis=1)
    starts = ends - counts
    e_idx = ri[:, :TOP_K]
    rank = ri[:, TOP_K:2 * TOP_K]
    onehot = e_idx[:, :, None] == ex[None, None, :]
    pos = rank + jnp.sum(jnp.where(onehot, starts[None, None, :], 0), axis=-1)
    pos = pos.T.reshape(-1).astype(I32)
    n_tiles = n_tok * TOP_K // TM
    row0 = jnp.arange(n_tiles, dtype=I32) * TM
    ef = jnp.sum((ends[None, :] <= row0[:, None]).astype(I32), axis=1)
    el = jnp.sum((ends[None, :] <= row0[:, None] + (TM - 1)).astype(I32), axis=1)
    later = (ex[None, :] > ex[:, None]) & (counts[None, :] > 0)
    nxt = jnp.min(jnp.where(later, ex[None, :], N_EXPERTS), axis=1)
    starts33 = jnp.concatenate([starts, ends[-1:]])
    return pos, starts33.astype(I32), nxt.astype(I32), ef.astype(I32), el.astype(I32)


def _moe(x, h, ri, rg, cnt, mod3, layer, w_gu, b_gu, w_dn, b_dn, g_final, final, npt, tps):
    pos, starts, nxt, ef, el = _routing_tables(ri, cnt, x.shape[0])
    yk = _gmm_call(pos, starts, nxt, ef, el, h, w_gu, w_dn, b_gu, b_dn, layer)
    return _combine_call(yk, x, rg, mod3, layer, g_final, final, npt, tps)


def _rope_tables(n_tokens):
    n_rows = n_tokens // GRID_W
    rows = jnp.repeat(jnp.arange(n_rows, dtype=F32), GRID_W)
    cols = jnp.tile(jnp.arange(GRID_W, dtype=F32), n_rows)
    axis_dim = A_HEAD_DIM // 2
    inv = ROPE_THETA ** (-jnp.arange(0, axis_dim, 2, dtype=F32) / axis_dim)
    ang_r = rows[:, None] * inv[None, :]
    ang_c = cols[:, None] * inv[None, :]
    ang = jnp.concatenate([ang_r, ang_r, ang_c, ang_c], axis=-1)
    cos, sin = jnp.cos(ang), jnp.sin(ang)
    sign = jnp.where((jnp.arange(A_HEAD_DIM) % 32) < 16, -1.0, 1.0).astype(F32)
    sin = sin * sign[None, :]
    cos = jnp.concatenate([jnp.ones((TM, A_HEAD_DIM), F32), cos], axis=0)
    sin = jnp.concatenate([jnp.zeros((TM, A_HEAD_DIM), F32), sin], axis=0)
    return jnp.tile(cos, (1, LANES // A_HEAD_DIM)), jnp.tile(sin, (1, LANES // A_HEAD_DIM))


def kernel(x_prompt, x_sample, cache_diff_k, cache_diff_v, cache_gqa_k, cache_gqa_v, c, c_ctx, w_ada, b_ada, norm_mix, norm_ffn, norm_final, w_attn_in, w_attn_out, lam_q1, lam_k1, lam_q2, lam_k2, diff_subln, gqa_q_norm, gqa_k_norm, w_pool, pool_scale, w_router, b_router, w_gate_up, b_gate_up, w_down, b_down):
    n_prompt, prompt_len, d = x_prompt.shape
    n_sample, sample_len, _ = x_sample.shape
    assert d == D_MODEL and prompt_len % TM == 0 and sample_len % TM == 0
    assert n_sample + 1 <= SUBLANES
    npt = n_prompt * prompt_len // TM
    tps = sample_len // TM
    tp = n_prompt * prompt_len
    past = cache_diff_k.shape[2]

    x = jnp.concatenate([x_prompt.reshape(tp, d), x_sample.reshape(n_sample * sample_len, d)], axis=0)
    cond8 = jnp.zeros((SUBLANES, d), F32).at[0].set(c_ctx).at[1:1 + n_sample].set(c)
    mod = _modulation_all(cond8, w_ada, b_ada)
    mod3 = mod[:, :1 + n_sample].reshape(DEPTH * (1 + n_sample), 1, N_MOD * d)
    assert n_sample == 2

    cos_t, sin_t = _rope_tables(sample_len)
    seg_r = jnp.arange(BQ_W)[:, None] // B_HEAD_DIM
    seg = (seg_r == seg_r.T).astype(BF16)

    caches = []
    y_final = None
    for i in range(DEPTH):
        j = i // 2
        g_mix = norm_mix[i][None]
        g_ffn = norm_ffn[i][None]
        wr = jnp.zeros((d, LANES), F32).at[:, :N_EXPERTS].set(w_router[i])
        br = jnp.full((1, LANES), NEG_BIG, F32).at[0, :N_EXPERTS].set(b_router[i])
        if i % 2 == 0:
            lambda_init = 0.8 - 0.6 * math.exp(-0.3 * i)
            gq = jnp.tile(gqa_q_norm[j], BQ_W // B_HEAD_DIM)[None]
            gk = jnp.tile(gqa_k_norm[j], BKV_W // B_HEAD_DIM)[None]
            qa, ka, va, qb, kbd, vb, ck, cv, cgk, cgv = _qkv_call(
                x, mod3, i, g_mix, w_attn_in[j].astype(BF16), gq, gk, cos_t, sin_t, seg, npt, tps)
            caches.append((ck, cv, cgk, cgv))
            cka = cache_diff_k[:, j].reshape(n_sample, past, A_W).astype(BF16)
            cva = cache_diff_v[:, j].reshape(n_sample, past, A_W).astype(BF16)
            gk_c = cache_gqa_k[:, j]
            ckbd = jnp.concatenate([gk_c[:, :, 0], gk_c[:, :, 0], gk_c[:, :, 1], gk_c[:, :, 1]],
                                   axis=-1).astype(BF16)
            cvb = cache_gqa_v[:, j].reshape(n_sample, past, BKV_W).astype(BF16)
            o = _attention(lambda_init, qa, ka, va, qb, kbd, vb, cka, cva, ckbd, cvb,
                           lam_q1[j][None], lam_k1[j][None], lam_q2[j][None], lam_k2[j][None],
                           diff_subln[j][None], n_prompt, prompt_len, n_sample, sample_len)
            x, h, ri, rg, cnt = _post_attn_call(x, o, mod3, i, w_attn_out[j].astype(BF16), g_ffn, wr, br, npt, tps)
        else:
            x, h, ri, rg, cnt = _pool_call(x, mod3, i, g_mix, w_pool[j].astype(BF16), pool_scale[j][None],
                                           g_ffn, wr, br, npt, tps)
        final = i == DEPTH - 1
        outs = _moe(x, h, ri, rg, cnt, mod3, i, w_gate_up, b_gate_up, w_down, b_down, norm_final[None], final, npt, tps)
        x = outs[0]
        if final:
            y_final = outs[1]

    y_prompt = y_final[:tp].reshape(n_prompt, prompt_len, d)
    y_sample = y_final[tp:].reshape(n_sample, sample_len, d)
    stack = lambda k, shp: jnp.stack([cc[k].reshape(shp) for cc in caches], axis=1)
    new_diff_k = stack(0, (n_prompt, prompt_len, A_HEADS, 2 * A_HEAD_DIM))
    new_diff_v = stack(1, (n_prompt, prompt_len, A_HEADS, 2 * A_HEAD_DIM))
    new_gqa_k = stack(2, (n_prompt, prompt_len, B_KV_HEADS, B_HEAD_DIM))
    new_gqa_v = stack(3, (n_prompt, prompt_len, B_KV_HEADS, B_HEAD_DIM))
    return (y_prompt, y_sample, new_diff_k, new_diff_v, new_gqa_k, new_gqa_v)
```

```python
import functools
import math

import jax
import jax.numpy as jnp
from jax import lax
from jax.experimental import pallas as pl
from jax.experimental.pallas import tpu as pltpu

F32 = jnp.float32
BF16 = jnp.bfloat16
I32 = jnp.int32
U32 = jnp.uint32

D_MODEL = 1024
DEPTH = 4
GRID_W = 64
A_HEADS = 4
A_HEAD_DIM = 64
B_Q_HEADS = 8
B_KV_HEADS = 2
B_HEAD_DIM = 64
ROPE_THETA = 10000.0
A_W = A_HEADS * 2 * A_HEAD_DIM
BQ_W = B_Q_HEADS * B_HEAD_DIM
BKV_W = B_KV_HEADS * B_HEAD_DIM
IN_W = 3 * A_W + BQ_W + 2 * BKV_W
POOL_WINDOWS = (2, 4, 8, 16)
POOL_GROUPS = 4
POOL_CH = D_MODEL // POOL_GROUPS
N_EXPERTS = 32
TOP_K = 4
D_EXPERT = D_MODEL
SWIGLU_LIMIT = 7.0
SWIGLU_ALPHA = 1.702
N_MOD = 6
EPS = 1e-6

LANES = 128
SUBLANES = 8
TM = 256
ROW_TILES = D_MODEL // LANES
HALO = 8
ADA_TN = 1536
FF_CHUNK = 512
VMEM_LIMIT = 56 * 1024 * 1024
NEG_BIG = -1e30


def _cparams(n_axes, vmem=VMEM_LIMIT):
    return pltpu.CompilerParams(
        dimension_semantics=("arbitrary",) * n_axes, vmem_limit_bytes=vmem)


def _dot(a, b):
    return jnp.dot(a, b, preferred_element_type=F32)


def _dot_nt(a, b):
    return lax.dot_general(a, b, (((1,), (1,)), ((), ())), preferred_element_type=F32)


def _rms(x, g):
    ms = jnp.mean(x * x, axis=-1, keepdims=True)
    return x * lax.rsqrt(ms + EPS) * g


def _lane_iota(shape):
    return lax.broadcasted_iota(I32, shape, len(shape) - 1)


def _store_token_tiles(ref, x, mask=None):
    n = x.shape[0]
    for c in range(ROW_TILES):
        idx = (pl.ds(c, n, stride=ROW_TILES), slice(None))
        v = x[:, c * LANES:(c + 1) * LANES]
        ref[idx] = v if mask is None else jnp.where(mask, v, ref[idx])


def _load_token_tiles(ref, n):
    return jnp.concatenate([ref[pl.ds(c, n, stride=ROW_TILES), :] for c in range(ROW_TILES)], axis=1)


def _ada_kernel(c_ref, w_ref, b_ref, o_ref):
    c = c_ref[...]
    s = (c * jax.nn.sigmoid(c)).astype(BF16)
    o_ref[0] = _dot(s, w_ref[0].astype(BF16)) + b_ref[0]


def _modulation_all(cond8, w_ada, b_ada):
    nmod = w_ada.shape[-1]
    return pl.pallas_call(
        _ada_kernel,
        grid=(DEPTH, nmod // ADA_TN),
        in_specs=[
            pl.BlockSpec((SUBLANES, D_MODEL), lambda l, n: (0, 0)),
            pl.BlockSpec((1, D_MODEL, ADA_TN), lambda l, n: (l, 0, n)),
            pl.BlockSpec((1, 1, ADA_TN), lambda l, n: (l, 0, n)),
        ],
        out_specs=pl.BlockSpec((1, SUBLANES, ADA_TN), lambda l, n: (l, 0, n)),
        out_shape=jax.ShapeDtypeStruct((DEPTH, SUBLANES, nmod), F32),
        compiler_params=_cparams(2),
        name="modulation",
    )(cond8, w_ada, b_ada.reshape(DEPTH, 1, nmod))


def _seg_meansq(x, seg):
    x2 = x * x
    hi = x2.astype(BF16)
    lo = (x2 - hi.astype(F32)).astype(BF16)
    return (_dot(hi, seg) + _dot(lo, seg)) * (1.0 / B_HEAD_DIM)


def _rope128(x, cos, sin_signed):
    lo16 = (_lane_iota(x.shape) % 32) < 16
    nxt = pltpu.roll(x, LANES - 16, 1)
    prv = pltpu.roll(x, 16, 1)
    return x * cos + jnp.where(lo16, nxt, prv) * sin_signed


def _rope(x, cos, sin_signed):
    cols = [_rope128(x[:, c:c + LANES], cos, sin_signed) for c in range(0, x.shape[1], LANES)]
    return cols[0] if len(cols) == 1 else jnp.concatenate(cols, axis=1)


def _qkv_kernel(n_prompt_tiles, x_ref, mod_ref, g_ref, w_ref, gq_ref, gk_ref, cos_ref, sin_ref,
                seg_ref, qa_ref, ka_ref, va_ref, qb_ref, kbd_ref, vb_ref,
                ck_ref, cv_ref, cgk_ref, cgv_ref):
    i = pl.program_id(0)
    mod = mod_ref[0]
    shift = mod[:, 0:D_MODEL]
    scale = mod[:, D_MODEL:2 * D_MODEL]
    h = _rms(x_ref[...], g_ref[...]) * (1.0 + scale) + shift
    p = _dot(h.astype(BF16), w_ref[...])
    a_q = p[:, 0:A_W]
    a_k = p[:, A_W:2 * A_W]
    a_v = p[:, 2 * A_W:3 * A_W]
    o = 3 * A_W
    b_q = p[:, o:o + BQ_W]
    b_k = p[:, o + BQ_W:o + BQ_W + BKV_W]
    b_v = p[:, o + BQ_W + BKV_W:o + BQ_W + 2 * BKV_W]
    seg = seg_ref[...]
    b_q = b_q * lax.rsqrt(_seg_meansq(b_q, seg) + EPS) * gq_ref[...]
    b_k = b_k * lax.rsqrt(_seg_meansq(b_k, seg[:BKV_W, :BKV_W]) + EPS) * gk_ref[...]

    @pl.when(i < n_prompt_tiles)
    def _():
        ck_ref[...] = a_k
        cv_ref[...] = a_v
        cgk_ref[...] = b_k
        cgv_ref[...] = b_v

    cos = cos_ref[...]
    sin = sin_ref[...]
    sm = A_HEAD_DIM ** -0.5
    qa_ref[...] = (_rope(a_q, cos, sin) * sm).astype(BF16)
    ka_ref[...] = _rope(a_k, cos, sin).astype(BF16)
    va_ref[...] = a_v.astype(BF16)
    qb_ref[...] = (_rope(b_q, cos, sin) * (B_HEAD_DIM ** -0.5)).astype(BF16)
    kb = _rope(b_k, cos, sin)
    kb_sw = pltpu.roll(kb, B_HEAD_DIM, 1)
    lo = _lane_iota(kb.shape) < B_HEAD_DIM
    kbd_ref[...] = jnp.concatenate(
        [jnp.where(lo, kb, kb_sw), jnp.where(lo, kb_sw, kb)], axis=1).astype(BF16)
    vb_ref[...] = b_v.astype(BF16)


def _cond_row(i, npt, tiles_per_seq):
    return jnp.where(i < npt, 0, 1 + (i - npt) // tiles_per_seq)


def _qkv_call(x, mod3, layer, g_mix, w_in_bf, gq, gk, cos_t, sin_t, seg, npt, tps):
    t = x.shape[0]
    nt = t // TM
    tp = npt * TM
    row = lambda w: pl.BlockSpec((TM, w), lambda i: (i, 0))
    full = lambda a: pl.BlockSpec(a.shape, lambda i: (0,) * a.ndim)
    tab = pl.BlockSpec((TM, LANES), lambda i: (jnp.where(i < npt, 0, 1 + (i - npt) % tps), 0))
    cache = lambda w: pl.BlockSpec((TM, w), lambda i: (jnp.minimum(i, npt - 1), 0))
    return pl.pallas_call(
        functools.partial(_qkv_kernel, npt),
        grid=(nt,),
        in_specs=[
            row(D_MODEL),
            pl.BlockSpec((1, 1, N_MOD * D_MODEL), lambda i: (layer * 3 + _cond_row(i, npt, tps), 0, 0)),
            full(g_mix), full(w_in_bf), full(gq), full(gk), tab, tab, full(seg),
        ],
        out_specs=[row(A_W), row(A_W), row(A_W), row(BQ_W), row(2 * BKV_W), row(BKV_W),
                   cache(A_W), cache(A_W), cache(BKV_W), cache(BKV_W)],
        out_shape=[
            jax.ShapeDtypeStruct((t, A_W), BF16), jax.ShapeDtypeStruct((t, A_W), BF16),
            jax.ShapeDtypeStruct((t, A_W), BF16), jax.ShapeDtypeStruct((t, BQ_W), BF16),
            jax.ShapeDtypeStruct((t, 2 * BKV_W), BF16), jax.ShapeDtypeStruct((t, BKV_W), BF16),
            jax.ShapeDtypeStruct((tp, A_W), F32), jax.ShapeDtypeStruct((tp, A_W), F32),
            jax.ShapeDtypeStruct((tp, BKV_W), F32), jax.ShapeDtypeStruct((tp, BKV_W), F32),
        ],
        compiler_params=_cparams(1),
        name="attn_qkv",
    )(x, mod3, g_mix, w_in_bf, gq, gk, cos_t, sin_t, seg)


def _softmax_pv(qq, ks, vs):
    ss = [_dot_nt(qq, k) for k in ks]
    m = ss[0].max(axis=-1, keepdims=True)
    for s in ss[1:]:
        m = jnp.maximum(m, s.max(axis=-1, keepdims=True))
    acc = None
    l = None
    for s, v in zip(ss, vs):
        p = jnp.exp(s - m)
        ls = p.sum(axis=-1, keepdims=True)
        pv = _dot(p.astype(BF16), v)
        acc = pv if acc is None else acc + pv
        l = ls if l is None else l + ls
    return acc / l


def _split_halves(q):
    lo = _lane_iota(q.shape) < (LANES // 2)
    zero = jnp.zeros_like(q)
    return jnp.concatenate([jnp.where(lo, q, zero), jnp.where(lo, zero, q)], axis=0)


def _attn_body(lambda_init, qa, qb, ka_segs, va_segs, kbd_segs, vb_segs, lam, subln):
    tq = qa.shape[0]
    cols = []
    for h in range(A_HEADS):
        sl = slice(h * LANES, (h + 1) * LANES)
        o = _softmax_pv(_split_halves(qa[:, sl]), [k[:, sl] for k in ka_segs],
                        [v[:, sl] for v in va_segs])
        od = o[:tq] - lam * o[tq:]
        cols.append(_rms(od, subln) * (1.0 - lambda_init))
    lo = _lane_iota((tq, LANES)) < B_HEAD_DIM
    for c in range(BQ_W // LANES):
        g = (2 * c) // (B_Q_HEADS // B_KV_HEADS)
        sl = slice(c * LANES, (c + 1) * LANES)
        gl = slice(g * LANES, (g + 1) * LANES)
        o = _softmax_pv(_split_halves(qb[:, sl]), [k[:, gl] for k in kbd_segs], vb_segs)
        oe, oo = o[:tq], o[tq:]
        if g == 0:
            cols.append(jnp.where(lo, oe, pltpu.roll(oo, B_HEAD_DIM, 1)))
        else:
            cols.append(jnp.where(lo, pltpu.roll(oe, B_HEAD_DIM, 1), oo))
    return jnp.concatenate(cols, axis=1).astype(BF16)


def _lambda(lq1, lk1, lq2, lk2, lambda_init):
    return (jnp.exp(jnp.sum(lq1 * lk1, axis=-1, keepdims=True))
            - jnp.exp(jnp.sum(lq2 * lk2, axis=-1, keepdims=True)) + lambda_init)


def _attn_prompt_kernel(lambda_init, qa, ka, va, qb, kbd, vb, lq1, lk1, lq2, lk2, subln, o_ref):
    lam = _lambda(lq1[...], lk1[...], lq2[...], lk2[...], lambda_init)
    o_ref[...] = _attn_body(lambda_init, qa[...], qb[...], [ka[...]], [va[...]],
                            [kbd[...]], [vb[...]], lam, subln[...])


def _attn_sample_kernel(lambda_init, qa, ka, va, qb, kbd, vb, cka, cva, ckbd, cvb,
                        lq1, lk1, lq2, lk2, subln, o_ref):
    lam = _lambda(lq1[...], lk1[...], lq2[...], lk2[...], lambda_init)
    o_ref[...] = _attn_body(lambda_init, qa[...], qb[...], [ka[...], cka[...]], [va[...], cva[...]],
                            [kbd[...], ckbd[...]], [vb[...], cvb[...]], lam, subln[...])


def _attention(lambda_init, qa, ka, va, qb, kbd, vb, cka, cva, ckbd, cvb,
               lq1, lk1, lq2, lk2, subln, n_prompt, prompt_len, n_sample, sample_len):
    npt = n_prompt * prompt_len // TM
    small = [lq1, lk1, lq2, lk2, subln]
    widths = [A_W, A_W, A_W, BQ_W, 2 * BKV_W, BKV_W]
    assert prompt_len == TM
    full1 = lambda a: pl.BlockSpec(a.shape, lambda b: (0,) * a.ndim)
    o_p = pl.pallas_call(
        functools.partial(_attn_prompt_kernel, lambda_init),
        grid=(n_prompt,),
        in_specs=[pl.BlockSpec((TM, w), lambda b: (b, 0)) for w in widths] + [full1(a) for a in small],
        out_specs=pl.BlockSpec((TM, D_MODEL), lambda b: (b, 0)),
        out_shape=jax.ShapeDtypeStruct((npt * TM, D_MODEL), BF16),
        compiler_params=_cparams(1),
        name="attn_prompt",
    )(qa, ka, va, qb, kbd, vb, *small)
    tps = sample_len // TM
    seq0 = n_prompt * prompt_len // sample_len
    assert seq0 * sample_len == n_prompt * prompt_len
    past = cka.shape[1]
    qspec = lambda w: pl.BlockSpec((TM, w), lambda b, q: (npt + b * tps + q, 0))
    kspec = lambda w: pl.BlockSpec((sample_len, w), lambda b, q: (seq0 + b, 0))
    cspec = lambda w: pl.BlockSpec((None, past, w), lambda b, q: (b, 0, 0))
    full2 = lambda a: pl.BlockSpec(a.shape, lambda b, q: (0,) * a.ndim)
    o_s = pl.pallas_call(
        functools.partial(_attn_sample_kernel, lambda_init),
        grid=(n_sample, tps),
        in_specs=[qspec(A_W), kspec(A_W), kspec(A_W), qspec(BQ_W), kspec(2 * BKV_W), kspec(BKV_W),
                  cspec(A_W), cspec(A_W), cspec(2 * BKV_W), cspec(BKV_W)] + [full2(a) for a in small],
        out_specs=pl.BlockSpec((TM, D_MODEL), lambda b, q: (b * tps + q, 0)),
        out_shape=jax.ShapeDtypeStruct((n_sample * sample_len, D_MODEL), BF16),
        compiler_params=_cparams(2),
        name="attn_sample",
    )(qa, ka, va, qb, kbd, vb, cka, cva, ckbd, cvb, *small)
    return o_p, o_s


def _router_tail(i, x, y, mod, g_ffn, wr, br, xo_ref, h_ref, ri_ref, rg_ref, cnt_ref, carry_ref):
    gate_mix = mod[:, 2 * D_MODEL:3 * D_MODEL]
    shift = mod[:, 3 * D_MODEL:4 * D_MODEL]
    scale = mod[:, 4 * D_MODEL:5 * D_MODEL]
    xn = x + gate_mix * y
    xo_ref[...] = xn
    h = _rms(xn, g_ffn) * (1.0 + scale) + shift
    _store_token_tiles(h_ref, h)
    h_hi = h.astype(BF16)
    h_lo = (h - h_hi.astype(F32)).astype(BF16)
    w_hi = wr.astype(BF16)
    w_lo = (wr - w_hi.astype(F32)).astype(BF16)
    logits = _dot(h_hi, w_hi) + _dot(h_lo, w_hi) + _dot(h_hi, w_lo) + br
    lane = _lane_iota(logits.shape)
    lane_f = lane.astype(F32)
    vals, idxs = [], []
    l = logits
    for _ in range(TOP_K):
        m = l.max(axis=-1, keepdims=True)
        idx = jnp.where(l == m, lane_f, float(LANES)).min(axis=-1, keepdims=True)
        vals.append(m)
        idxs.append(idx)
        l = jnp.where(lane_f == idx, -jnp.inf, l)
    es = [jnp.exp(v - vals[0]) for v in vals]
    den = es[0]
    for e in es[1:]:
        den = den + e
    sel = jnp.zeros(logits.shape, F32)
    for idx in idxs:
        sel = sel + jnp.where(lane_f == idx, 1.0, 0.0)

    @pl.when(i == 0)
    def _():
        carry_ref[...] = jnp.zeros_like(carry_ref)

    carry = carry_ref[...]
    r_io = lax.broadcasted_iota(I32, (TM, TM), 0)
    c_io = lax.broadcasted_iota(I32, (TM, TM), 1)
    tri = jnp.where(c_io < r_io, 1.0, 0.0).astype(BF16)
    rank = _dot(tri, sel.astype(BF16)) + carry
    carry = carry + sel.sum(axis=0, keepdims=True)
    carry_ref[...] = carry
    cnt_ref[...] = carry
    ri = jnp.zeros(logits.shape, F32)
    rg = jnp.zeros(logits.shape, F32)
    for k in range(TOP_K):
        rk = jnp.where(lane_f == idxs[k], rank, 0.0).sum(axis=-1, keepdims=True)
        ri = jnp.where(lane == k, idxs[k], ri)
        ri = jnp.where(lane == TOP_K + k, rk, ri)
        rg = jnp.where(lane == k, es[k] / den, rg)
    ri_ref[...] = ri.astype(I32)
    rg_ref[...] = rg


def _tail_specs(t):
    row = lambda w: pl.BlockSpec((TM, w), lambda i: (i, 0))
    out_specs = [row(D_MODEL), pl.BlockSpec((TM * ROW_TILES, LANES), lambda i: (i, 0)), row(LANES), row(LANES),
                 pl.BlockSpec((1, LANES), lambda i: (0, 0))]
    out_shape = [jax.ShapeDtypeStruct((t, D_MODEL), F32), jax.ShapeDtypeStruct((t * ROW_TILES, LANES), F32),
                 jax.ShapeDtypeStruct((t, LANES), I32), jax.ShapeDtypeStruct((t, LANES), F32),
                 jax.ShapeDtypeStruct((1, LANES), F32)]
    return out_specs, out_shape


def _post_attn_kernel(npt, x_ref, op_ref, os_ref, mod_ref, w_ref, g_ref, wr_ref, br_ref,
                      xo_ref, h_ref, ri_ref, rg_ref, cnt_ref, carry_ref):
    i = pl.program_id(0)
    y = _dot(jnp.where(i < npt, op_ref[...], os_ref[...]), w_ref[...])
    _router_tail(i, x_ref[...], y, mod_ref[0], g_ref[...], wr_ref[...], br_ref[...],
                 xo_ref, h_ref, ri_ref, rg_ref, cnt_ref, carry_ref)


def _post_attn_call(x, o_p, o_s, mod3, layer, w_out_bf, g_ffn, wr, br, npt, tps):
    t = x.shape[0]
    row = lambda w: pl.BlockSpec((TM, w), lambda i: (i, 0))
    full = lambda a: pl.BlockSpec(a.shape, lambda i: (0,) * a.ndim)
    out_specs, out_shape = _tail_specs(t)
    return pl.pallas_call(
        functools.partial(_post_attn_kernel, npt),
        grid=(t // TM,),
        in_specs=[row(D_MODEL),
                  pl.BlockSpec((TM, D_MODEL), lambda i: (jnp.minimum(i, npt - 1), 0)),
                  pl.BlockSpec((TM, D_MODEL), lambda i: (jnp.maximum(i - npt, 0), 0)),
                  pl.BlockSpec((1, 1, N_MOD * D_MODEL), lambda i: (layer * 3 + _cond_row(i, npt, tps), 0, 0)),
                  full(w_out_bf), full(g_ffn), full(wr), full(br)],
        out_specs=out_specs, out_shape=out_shape,
        scratch_shapes=[pltpu.VMEM((1, LANES), F32)],
        compiler_params=_cparams(1),
        name="attn_out_router",
    )(x, o_p, o_s, mod3, w_out_bf, g_ffn, wr, br)


def _pool_kernel(npt, tps, x_ref, xp_ref, xn_ref, mod_ref, gm_ref, wp_ref, ps_ref, g_ref, wr_ref, br_ref,
                 xo_ref, h_ref, ri_ref, rg_ref, cnt_ref, carry_ref):
    i = pl.program_id(0)
    mod = mod_ref[0]
    shift = mod[:, 0:D_MODEL]
    scale = mod[:, D_MODEL:2 * D_MODEL]
    gm = gm_ref[...]
    x = x_ref[...]
    pre = lambda v: _rms(v, gm) * (1.0 + scale) + shift
    j = jnp.where(i < npt, 0, (i - npt) % tps)
    ntile = jnp.where(i < npt, 1, tps)
    has_prev = (j > 0).astype(F32)
    has_next = (j < ntile - 1).astype(F32)
    h = pre(x)
    hc = jnp.concatenate([pre(xp_ref[...]) * has_prev, h, pre(xn_ref[...]) * has_next], axis=0)
    rows = hc.shape[0]
    t_seq = (j * TM + lax.broadcasted_iota(I32, (TM, 1), 0)).astype(F32)
    seq_len = (ntile * TM).astype(F32)
    ys = []
    for g, w in enumerate(POOL_WINDOWS):
        half = w // 2
        s = hc[:, g * POOL_CH:(g + 1) * POOL_CH]
        step = 1
        while step < w:
            s = s + pltpu.roll(s, rows - step, 0)
            step *= 2
        s = pltpu.roll(s, half, 0) if half != HALO else s
        win = s[HALO:HALO + TM] if half != HALO else s[0:TM]
        cnt = jnp.minimum(t_seq + half, seq_len) - jnp.maximum(t_seq - half, 0.0)
        pooled = win / cnt - h[:, g * POOL_CH:(g + 1) * POOL_CH]
        ys.append(_dot(pooled.astype(BF16), wp_ref[g]))
    y = jnp.concatenate(ys, axis=1) * ps_ref[...]
    _router_tail(i, x, y, mod, g_ref[...], wr_ref[...], br_ref[...],
                 xo_ref, h_ref, ri_ref, rg_ref, cnt_ref, carry_ref)


def _pool_call(x, mod3, layer, g_mix, w_pool_bf, pool_scale, g_ffn, wr, br, npt, tps):
    t = x.shape[0]
    per = TM // HALO
    nh = t // HALO
    row = lambda w: pl.BlockSpec((TM, w), lambda i: (i, 0))
    full = lambda a: pl.BlockSpec(a.shape, lambda i: (0,) * a.ndim)
    out_specs, out_shape = _tail_specs(t)
    return pl.pallas_call(
        functools.partial(_pool_kernel, npt, tps),
        grid=(t // TM,),
        in_specs=[row(D_MODEL),
                  pl.BlockSpec((HALO, D_MODEL), lambda i: (jnp.maximum(i * per - 1, 0), 0)),
                  pl.BlockSpec((HALO, D_MODEL), lambda i: (jnp.minimum((i + 1) * per, nh - 1), 0)),
                  pl.BlockSpec((1, 1, N_MOD * D_MODEL), lambda i: (layer * 3 + _cond_row(i, npt, tps), 0, 0)),
                  full(g_mix), full(w_pool_bf), full(pool_scale), full(g_ffn), full(wr), full(br)],
        out_specs=out_specs, out_shape=out_shape,
        scratch_shapes=[pltpu.VMEM((1, LANES), F32)],
        compiler_params=_cparams(1),
        name="pool_router",
    )(x, x, x, mod3, g_mix, w_pool_bf, pool_scale, g_ffn, wr, br)


def _expert_rows(x, e, wgu_bf, wd_bf, bgu_ref, bd_ref):
    gu = _dot(x, wgu_bf[...]) + bgu_ref[pl.ds(e, 1), :]
    gate = jnp.minimum(gu[:, :D_EXPERT], SWIGLU_LIMIT)
    up = jnp.clip(gu[:, D_EXPERT:], -SWIGLU_LIMIT, SWIGLU_LIMIT)
    act = gate * jax.nn.sigmoid(SWIGLU_ALPHA * gate) * (up + 1.0)
    return _dot(act.astype(BF16), wd_bf[...]) + bd_ref[pl.ds(e, 1), :]


ROW_GROUP = 8


def _for_rows(n, fn):
    def group(g, c):
        for q in range(ROW_GROUP):
            fn(g * ROW_GROUP + q, q % 2)
        return c

    def single(r, c):
        fn(r, 0)
        return c
    full = n // ROW_GROUP
    lax.fori_loop(0, full, group, 0)
    lax.fori_loop(full * ROW_GROUP, n, single, 0)


def _gmm_kernel(layer, n_tok, pos_ref, te_ref, nv_ref, nxt_ref,
                h_hbm, wgu_hbm, wd_hbm, bgu_ref, bd_ref, yk_hbm,
                src_ref, cur_ref, xbuf, ybuf, wgu_st, wd_st, wgu_bf, wd_bf, sem_g, sem_s, sem_w):
    w = pl.program_id(0)
    n_steps = pl.num_programs(0)
    slot = w % 2
    nv = nv_ref[w]
    nv_next = jnp.where(w + 1 < n_steps, nv_ref[jnp.minimum(w + 1, n_steps - 1)], 0)

    def weights_copy(e):
        return (pltpu.make_async_copy(wgu_hbm.at[layer, e], wgu_st, sem_w.at[0]),
                pltpu.make_async_copy(wd_hbm.at[layer, e], wd_st, sem_w.at[1]))

    def tile_of(ref, start):
        return ref.at[pl.ds(pl.multiple_of(start, ROW_TILES), ROW_TILES), :]

    def rows_of(ref, n):
        return ref.at[pl.ds(0, pl.multiple_of(n * ROW_TILES, ROW_TILES)), :]

    def gather_start(tile, dst_slot, n):
        def one(r, prio):
            tok_row = src_ref[tile * TM + r] & (n_tok * ROW_TILES - 1)
            pltpu.make_async_copy(tile_of(h_hbm, tok_row), tile_of(xbuf.at[dst_slot], r * ROW_TILES),
                                  sem_g.at[dst_slot]).start(priority=prio)
        _for_rows(n, one)

    def gather_wait(dst_slot, n):
        pltpu.make_async_copy(rows_of(h_hbm, n), rows_of(xbuf.at[dst_slot], n), sem_g.at[dst_slot]).wait()

    def scatter_start(tile, src_slot, n):
        def one(r, prio):
            pltpu.make_async_copy(tile_of(ybuf.at[src_slot], r * ROW_TILES),
                                  tile_of(yk_hbm, src_ref[tile * TM + r]),
                                  sem_s.at[src_slot]).start(priority=prio)
        _for_rows(n, one)

    def scatter_wait(src_slot, n):
        pltpu.make_async_copy(rows_of(ybuf.at[src_slot], n), rows_of(yk_hbm, n), sem_s.at[src_slot]).wait()

    @pl.when(w == 0)
    def _():
        def invert(a, c):
            src_ref[pos_ref[a]] = a * ROW_TILES
            return c
        lax.fori_loop(0, n_tok * TOP_K, invert, 0, unroll=8)
        xbuf[...] = jnp.zeros_like(xbuf)
        cur_ref[0] = -1
        for cp in weights_copy(te_ref[0]):
            cp.start()
        gather_start(0, 0, nv)

    @pl.when(nv > 0)
    def _():
        e = te_ref[w]
        gather_wait(slot, nv)

        @pl.when(nv_next > 0)
        def _():
            gather_start(w + 1, 1 - slot, nv_next)

        @pl.when(w >= 2)
        def _():
            scatter_wait(slot, nv_ref[jnp.maximum(w - 2, 0)])

        @pl.when(cur_ref[0] != e)
        def _():
            for cp in weights_copy(e):
                cp.wait()
            wgu_bf[...] = wgu_st[...].astype(BF16)
            wd_bf[...] = wd_st[...].astype(BF16)
            cur_ref[0] = e

            @pl.when(nxt_ref[e] < N_EXPERTS)
            def _():
                for cp in weights_copy(nxt_ref[e]):
                    cp.start()

        x = _load_token_tiles(xbuf.at[slot], TM).astype(BF16)
        _store_token_tiles(ybuf.at[slot], _expert_rows(x, e, wgu_bf, wd_bf, bgu_ref, bd_ref))
        scatter_start(w, slot, nv)

        @pl.when(nv_next == 0)
        def _():
            @pl.when(w >= 1)
            def _():
                scatter_wait(1 - slot, nv_ref[jnp.maximum(w - 1, 0)])
            scatter_wait(slot, nv)


def _gmm_call(pos, te, nv, nxt, h, w_gu, w_dn, b_gu, b_dn, layer):
    assert ROW_TILES == SUBLANES
    t = h.shape[0] // ROW_TILES
    assert t & (t - 1) == 0
    n_rows = t * TOP_K
    n_steps = te.shape[0]
    return pl.pallas_call(
        functools.partial(_gmm_kernel, layer, t),
        grid_spec=pltpu.PrefetchScalarGridSpec(
            num_scalar_prefetch=4, grid=(n_steps,),
            in_specs=[pl.BlockSpec(memory_space=pl.ANY), pl.BlockSpec(memory_space=pl.ANY),
                      pl.BlockSpec(memory_space=pl.ANY),
                      pl.BlockSpec((None, N_EXPERTS, 2 * D_EXPERT), lambda w, *_: (layer, 0, 0)),
                      pl.BlockSpec((None, N_EXPERTS, D_MODEL), lambda w, *_: (layer, 0, 0))],
            out_specs=pl.BlockSpec(memory_space=pl.ANY),
            scratch_shapes=[
                pltpu.SMEM((n_steps * TM,), I32), pltpu.SMEM((1,), I32),
                pltpu.VMEM((2, TM * ROW_TILES, LANES), F32), pltpu.VMEM((2, TM * ROW_TILES, LANES), F32),
                pltpu.VMEM((D_MODEL, 2 * D_EXPERT), F32), pltpu.VMEM((D_EXPERT, D_MODEL), F32),
                pltpu.VMEM((D_MODEL, 2 * D_EXPERT), BF16), pltpu.VMEM((D_EXPERT, D_MODEL), BF16),
                pltpu.SemaphoreType.DMA((2,)), pltpu.SemaphoreType.DMA((2,)), pltpu.SemaphoreType.DMA((2,))]),
        out_shape=jax.ShapeDtypeStruct((n_rows * ROW_TILES, LANES), F32),
        compiler_params=_cparams(1),
        name="moe_experts",
    )(pos, te, nv, nxt, h, w_gu, w_dn, b_gu, b_dn)


def _combine_kernel(final, npt, x_ref, rg_ref, mod_ref, gf_ref, y0_ref, y1_ref, y2_ref, y3_ref, *outs):
    i = pl.program_id(0)
    rg = rg_ref[...]
    moe = rg[:, 0:1] * _load_token_tiles(y0_ref, TM)
    for k, y_ref in enumerate((y1_ref, y2_ref, y3_ref), start=1):
        moe = moe + rg[:, k:k + 1] * _load_token_tiles(y_ref, TM)
    gate_ffn = mod_ref[0][:, 5 * D_MODEL:6 * D_MODEL]
    xn = x_ref[...] + gate_ffn * moe
    if not final:
        outs[0][...] = xn
        return
    y = _rms(xn, gf_ref[...])
    yp_ref, ys_ref = outs

    @pl.when(i < npt)
    def _():
        yp_ref[...] = y

    @pl.when(i >= npt)
    def _():
        ys_ref[...] = y


def _combine_call(yk, x, rg, mod3, layer, g_final, final, npt, tps):
    t = x.shape[0]
    nt = t // TM
    assert TOP_K == 4
    row = lambda w: pl.BlockSpec((TM, w), lambda i: (i, 0))
    slab = lambda k: pl.BlockSpec((TM * ROW_TILES, LANES), lambda i: (k * nt + i, 0))
    if final:
        out_specs = [pl.BlockSpec((TM, D_MODEL), lambda i: (jnp.minimum(i, npt - 1), 0)),
                     pl.BlockSpec((TM, D_MODEL), lambda i: (jnp.maximum(i - npt, 0), 0))]
        out_shape = [jax.ShapeDtypeStruct((npt * TM, D_MODEL), F32),
                     jax.ShapeDtypeStruct(((nt - npt) * TM, D_MODEL), F32)]
    else:
        out_specs = [row(D_MODEL)]
        out_shape = [jax.ShapeDtypeStruct((t, D_MODEL), F32)]
    return pl.pallas_call(
        functools.partial(_combine_kernel, final, npt),
        grid=(nt,),
        in_specs=[row(D_MODEL), row(LANES),
                  pl.BlockSpec((1, 1, N_MOD * D_MODEL), lambda i: (layer * 3 + _cond_row(i, npt, tps), 0, 0)),
                  pl.BlockSpec(g_final.shape, lambda i: (0, 0))] + [slab(k) for k in range(TOP_K)],
        out_specs=out_specs, out_shape=out_shape,
        compiler_params=_cparams(1),
        name="moe_combine",
    )(x, rg, mod3, g_final, yk, yk, yk, yk)


def _routing_tables(ri, cnt, n_tok):
    ex = jnp.arange(N_EXPERTS, dtype=I32)
    counts = cnt[0, :N_EXPERTS].astype(I32)
    tiles = (counts + (TM - 1)) // TM
    tile_end = jnp.sum(jnp.where(ex[None, :] <= ex[:, None], tiles[None, :], 0), axis=1)
    tile_start = tile_end - tiles
    e_idx = ri[:, :TOP_K]
    rank = ri[:, TOP_K:2 * TOP_K]
    onehot = e_idx[:, :, None] == ex[None, None, :]
    pos = rank + jnp.sum(jnp.where(onehot, tile_start[None, None, :] * TM, 0), axis=-1)
    pos = pos.T.reshape(-1).astype(I32)
    n_steps = n_tok * TOP_K // TM + N_EXPERTS
    w = jnp.arange(n_steps, dtype=I32)
    te = jnp.minimum(jnp.sum((tile_end[None, :] <= w[:, None]).astype(I32), axis=1), N_EXPERTS - 1)
    mine = te[:, None] == ex[None, :]
    left = jnp.sum(jnp.where(mine, (counts - (w[:, None] - tile_start[None, :]) * TM)[...], 0), axis=1)
    nv = jnp.where(w < tile_end[-1], jnp.clip(left, 0, TM), 0)
    later = (ex[None, :] > ex[:, None]) & (counts[None, :] > 0)
    nxt = jnp.min(jnp.where(later, ex[None, :], N_EXPERTS), axis=1)
    return pos, te.astype(I32), nv.astype(I32), nxt.astype(I32)


def _moe(x, h, ri, rg, cnt, mod3, layer, w_gu, b_gu, w_dn, b_dn, g_final, final, npt, tps):
    pos, te, nv, nxt = _routing_tables(ri, cnt, x.shape[0])
    yk = _gmm_call(pos, te, nv, nxt, h, w_gu, w_dn, b_gu, b_dn, layer)
    return _combine_call(yk, x, rg, mod3, layer, g_final, final, npt, tps)


def _rope_tables(n_tokens):
    n_rows = n_tokens // GRID_W
    rows = jnp.repeat(jnp.arange(n_rows, dtype=F32), GRID_W)
    cols = jnp.tile(jnp.arange(GRID_W, dtype=F32), n_rows)
    axis_dim = A_HEAD_DIM // 2
    inv = ROPE_THETA ** (-jnp.arange(0, axis_dim, 2, dtype=F32) / axis_dim)
    ang_r = rows[:, None] * inv[None, :]
    ang_c = cols[:, None] * inv[None, :]
    ang = jnp.concatenate([ang_r, ang_r, ang_c, ang_c], axis=-1)
    cos, sin = jnp.cos(ang), jnp.sin(ang)
    sign = jnp.where((jnp.arange(A_HEAD_DIM) % 32) < 16, -1.0, 1.0).astype(F32)
    sin = sin * sign[None, :]
    cos = jnp.concatenate([jnp.ones((TM, A_HEAD_DIM), F32), cos], axis=0)
    sin = jnp.concatenate([jnp.zeros((TM, A_HEAD_DIM), F32), sin], axis=0)
    return jnp.tile(cos, (1, LANES // A_HEAD_DIM)), jnp.tile(sin, (1, LANES // A_HEAD_DIM))


def kernel(x_prompt, x_sample, cache_diff_k, cache_diff_v, cache_gqa_k, cache_gqa_v, c, c_ctx, w_ada, b_ada, norm_mix, norm_ffn, norm_final, w_attn_in, w_attn_out, lam_q1, lam_k1, lam_q2, lam_k2, diff_subln, gqa_q_norm, gqa_k_norm, w_pool, pool_scale, w_router, b_router, w_gate_up, b_gate_up, w_down, b_down):
    n_prompt, prompt_len, d = x_prompt.shape
    n_sample, sample_len, _ = x_sample.shape
    assert d == D_MODEL and prompt_len % TM == 0 and sample_len % TM == 0
    assert n_sample + 1 <= SUBLANES
    npt = n_prompt * prompt_len // TM
    tps = sample_len // TM
    tp = n_prompt * prompt_len
    past = cache_diff_k.shape[2]

    x = jnp.concatenate([x_prompt.reshape(tp, d), x_sample.reshape(n_sample * sample_len, d)], axis=0)
    cond8 = jnp.zeros((SUBLANES, d), F32).at[0].set(c_ctx).at[1:1 + n_sample].set(c)
    mod = _modulation_all(cond8, w_ada, b_ada)
    mod3 = mod[:, :1 + n_sample].reshape(DEPTH * (1 + n_sample), 1, N_MOD * d)
    assert n_sample == 2

    cos_t, sin_t = _rope_tables(sample_len)
    seg_r = jnp.arange(BQ_W)[:, None] // B_HEAD_DIM
    seg = (seg_r == seg_r.T).astype(BF16)

    caches = []
    y_final = None
    for i in range(DEPTH):
        j = i // 2
        g_mix = norm_mix[i][None]
        g_ffn = norm_ffn[i][None]
        wr = jnp.zeros((d, LANES), F32).at[:, :N_EXPERTS].set(w_router[i])
        br = jnp.full((1, LANES), NEG_BIG, F32).at[0, :N_EXPERTS].set(b_router[i])
        if i % 2 == 0:
            lambda_init = 0.8 - 0.6 * math.exp(-0.3 * i)
            gq = jnp.tile(gqa_q_norm[j], BQ_W // B_HEAD_DIM)[None]
            gk = jnp.tile(gqa_k_norm[j], BKV_W // B_HEAD_DIM)[None]
            qa, ka, va, qb, kbd, vb, ck, cv, cgk, cgv = _qkv_call(
                x, mod3, i, g_mix, w_attn_in[j].astype(BF16), gq, gk, cos_t, sin_t, seg, npt, tps)
            caches.append((ck, cv, cgk, cgv))
            cka = cache_diff_k[:, j].reshape(n_sample, past, A_W).astype(BF16)
            cva = cache_diff_v[:, j].reshape(n_sample, past, A_W).astype(BF16)
            gk_c = cache_gqa_k[:, j]
            ckbd = jnp.concatenate([gk_c[:, :, 0], gk_c[:, :, 0], gk_c[:, :, 1], gk_c[:, :, 1]],
                                   axis=-1).astype(BF16)
            cvb = cache_gqa_v[:, j].reshape(n_sample, past, BKV_W).astype(BF16)
            o_p, o_s = _attention(lambda_init, qa, ka, va, qb, kbd, vb, cka, cva, ckbd, cvb,
                                  lam_q1[j][None], lam_k1[j][None], lam_q2[j][None], lam_k2[j][None],
                                  diff_subln[j][None], n_prompt, prompt_len, n_sample, sample_len)
            x, h, ri, rg, cnt = _post_attn_call(x, o_p, o_s, mod3, i, w_attn_out[j].astype(BF16), g_ffn, wr, br,
                                                npt, tps)
        else:
            x, h, ri, rg, cnt = _pool_call(x, mod3, i, g_mix, w_pool[j].astype(BF16), pool_scale[j][None],
                                           g_ffn, wr, br, npt, tps)
        final = i == DEPTH - 1
        outs = _moe(x, h, ri, rg, cnt, mod3, i, w_gate_up, b_gate_up, w_down, b_down, norm_final[None], final, npt, tps)
        if final:
            y_final = outs
        else:
            x = outs[0]

    y_prompt = y_final[0].reshape(n_prompt, prompt_len, d)
    y_sample = y_final[1].reshape(n_sample, sample_len, d)
    stack = lambda k, shp: jnp.stack([cc[k].reshape(shp) for cc in caches], axis=1)
    new_diff_k = stack(0, (n_prompt, prompt_len, A_HEADS, 2 * A_HEAD_DIM))
    new_diff_v = stack(1, (n_prompt, prompt_len, A_HEADS, 2 * A_HEAD_DIM))
    new_gqa_k = stack(2, (n_prompt, prompt_len, B_KV_HEADS, B_HEAD_DIM))
    new_gqa_v = stack(3, (n_prompt, prompt_len, B_KV_HEADS, B_HEAD_DIM))
    return (y_prompt, y_sample, new_diff_k, new_diff_v, new_gqa_k, new_gqa_v)
```

```python
import functools
import math

import jax
import jax.numpy as jnp
from jax import lax
from jax.experimental import pallas as pl
from jax.experimental.pallas import tpu as pltpu

F32 = jnp.float32
BF16 = jnp.bfloat16
I32 = jnp.int32
U32 = jnp.uint32

D_MODEL = 1024
DEPTH = 4
GRID_W = 64
A_HEADS = 4
A_HEAD_DIM = 64
B_Q_HEADS = 8
B_KV_HEADS = 2
B_HEAD_DIM = 64
ROPE_THETA = 10000.0
A_W = A_HEADS * 2 * A_HEAD_DIM
BQ_W = B_Q_HEADS * B_HEAD_DIM
BKV_W = B_KV_HEADS * B_HEAD_DIM
IN_W = 3 * A_W + BQ_W + 2 * BKV_W
POOL_WINDOWS = (2, 4, 8, 16)
POOL_GROUPS = 4
POOL_CH = D_MODEL // POOL_GROUPS
N_EXPERTS = 32
TOP_K = 4
D_EXPERT = D_MODEL
SWIGLU_LIMIT = 7.0
SWIGLU_ALPHA = 1.702
N_MOD = 6
EPS = 1e-6

LANES = 128
SUBLANES = 8
TM = 256
ROW_TILES = D_MODEL // LANES
HALO = 8
ADA_TN = 1536
FF_CHUNK = 512
VMEM_LIMIT = 56 * 1024 * 1024
NEG_BIG = -1e30


def _cparams(n_axes, vmem=VMEM_LIMIT):
    return pltpu.CompilerParams(
        dimension_semantics=("arbitrary",) * n_axes, vmem_limit_bytes=vmem)


def _dot(a, b):
    return jnp.dot(a, b, preferred_element_type=F32)


def _dot_nt(a, b):
    return lax.dot_general(a, b, (((1,), (1,)), ((), ())), preferred_element_type=F32)


def _rms(x, g):
    ms = jnp.mean(x * x, axis=-1, keepdims=True)
    return x * lax.rsqrt(ms + EPS) * g


def _lane_iota(shape):
    return lax.broadcasted_iota(I32, shape, len(shape) - 1)


def _store_token_tiles(ref, x, mask=None):
    n = x.shape[0]
    for c in range(ROW_TILES):
        idx = (pl.ds(c, n, stride=ROW_TILES), slice(None))
        v = x[:, c * LANES:(c + 1) * LANES]
        ref[idx] = v if mask is None else jnp.where(mask, v, ref[idx])


def _load_token_tiles(ref, n):
    return jnp.concatenate([ref[pl.ds(c, n, stride=ROW_TILES), :] for c in range(ROW_TILES)], axis=1)


def _ada_kernel(c_ref, w_ref, b_ref, o_ref):
    c = c_ref[...]
    s = (c * jax.nn.sigmoid(c)).astype(BF16)
    o_ref[0] = _dot(s, w_ref[0].astype(BF16)) + b_ref[0]


def _modulation_all(cond8, w_ada, b_ada):
    nmod = w_ada.shape[-1]
    return pl.pallas_call(
        _ada_kernel,
        grid=(DEPTH, nmod // ADA_TN),
        in_specs=[
            pl.BlockSpec((SUBLANES, D_MODEL), lambda l, n: (0, 0)),
            pl.BlockSpec((1, D_MODEL, ADA_TN), lambda l, n: (l, 0, n)),
            pl.BlockSpec((1, 1, ADA_TN), lambda l, n: (l, 0, n)),
        ],
        out_specs=pl.BlockSpec((1, SUBLANES, ADA_TN), lambda l, n: (l, 0, n)),
        out_shape=jax.ShapeDtypeStruct((DEPTH, SUBLANES, nmod), F32),
        compiler_params=_cparams(2),
        name="modulation",
    )(cond8, w_ada, b_ada.reshape(DEPTH, 1, nmod))


def _seg_meansq(x, seg):
    x2 = x * x
    hi = x2.astype(BF16)
    lo = (x2 - hi.astype(F32)).astype(BF16)
    return (_dot(hi, seg) + _dot(lo, seg)) * (1.0 / B_HEAD_DIM)


def _rope128(x, cos, sin_signed):
    lo16 = (_lane_iota(x.shape) % 32) < 16
    nxt = pltpu.roll(x, LANES - 16, 1)
    prv = pltpu.roll(x, 16, 1)
    return x * cos + jnp.where(lo16, nxt, prv) * sin_signed


def _rope(x, cos, sin_signed):
    cols = [_rope128(x[:, c:c + LANES], cos, sin_signed) for c in range(0, x.shape[1], LANES)]
    return cols[0] if len(cols) == 1 else jnp.concatenate(cols, axis=1)


def _qkv_kernel(n_prompt_tiles, x_ref, mod_ref, g_ref, w_ref, gq_ref, gk_ref, cos_ref, sin_ref,
                seg_ref, qa_ref, ka_ref, va_ref, qb_ref, kbd_ref, vb_ref,
                ck_ref, cv_ref, cgk_ref, cgv_ref):
    i = pl.program_id(0)
    mod = mod_ref[0]
    shift = mod[:, 0:D_MODEL]
    scale = mod[:, D_MODEL:2 * D_MODEL]
    h = _rms(x_ref[...], g_ref[...]) * (1.0 + scale) + shift
    p = _dot(h.astype(BF16), w_ref[...])
    a_q = p[:, 0:A_W]
    a_k = p[:, A_W:2 * A_W]
    a_v = p[:, 2 * A_W:3 * A_W]
    o = 3 * A_W
    b_q = p[:, o:o + BQ_W]
    b_k = p[:, o + BQ_W:o + BQ_W + BKV_W]
    b_v = p[:, o + BQ_W + BKV_W:o + BQ_W + 2 * BKV_W]
    seg = seg_ref[...]
    b_q = b_q * lax.rsqrt(_seg_meansq(b_q, seg) + EPS) * gq_ref[...]
    b_k = b_k * lax.rsqrt(_seg_meansq(b_k, seg[:BKV_W, :BKV_W]) + EPS) * gk_ref[...]

    @pl.when(i < n_prompt_tiles)
    def _():
        ck_ref[...] = a_k
        cv_ref[...] = a_v
        cgk_ref[...] = b_k
        cgv_ref[...] = b_v

    cos = cos_ref[...]
    sin = sin_ref[...]
    sm = A_HEAD_DIM ** -0.5
    qa_ref[...] = (_rope(a_q, cos, sin) * sm).astype(BF16)
    ka_ref[...] = _rope(a_k, cos, sin).astype(BF16)
    va_ref[...] = a_v.astype(BF16)
    qb_ref[...] = (_rope(b_q, cos, sin) * (B_HEAD_DIM ** -0.5)).astype(BF16)
    kb = _rope(b_k, cos, sin)
    kb_sw = pltpu.roll(kb, B_HEAD_DIM, 1)
    lo = _lane_iota(kb.shape) < B_HEAD_DIM
    kbd_ref[...] = jnp.concatenate(
        [jnp.where(lo, kb, kb_sw), jnp.where(lo, kb_sw, kb)], axis=1).astype(BF16)
    vb_ref[...] = b_v.astype(BF16)


def _cond_row(i, npt, tiles_per_seq):
    return jnp.where(i < npt, 0, 1 + (i - npt) // tiles_per_seq)


def _qkv_call(x, mod3, layer, g_mix, w_in_bf, gq, gk, cos_t, sin_t, seg, npt, tps):
    t = x.shape[0]
    nt = t // TM
    tp = npt * TM
    row = lambda w: pl.BlockSpec((TM, w), lambda i: (i, 0))
    full = lambda a: pl.BlockSpec(a.shape, lambda i: (0,) * a.ndim)
    tab = pl.BlockSpec((TM, LANES), lambda i: (jnp.where(i < npt, 0, 1 + (i - npt) % tps), 0))
    cache = lambda w: pl.BlockSpec((TM, w), lambda i: (jnp.minimum(i, npt - 1), 0))
    return pl.pallas_call(
        functools.partial(_qkv_kernel, npt),
        grid=(nt,),
        in_specs=[
            row(D_MODEL),
            pl.BlockSpec((1, 1, N_MOD * D_MODEL), lambda i: (layer * 3 + _cond_row(i, npt, tps), 0, 0)),
            full(g_mix), full(w_in_bf), full(gq), full(gk), tab, tab, full(seg),
        ],
        out_specs=[row(A_W), row(A_W), row(A_W), row(BQ_W), row(2 * BKV_W), row(BKV_W),
                   cache(A_W), cache(A_W), cache(BKV_W), cache(BKV_W)],
        out_shape=[
            jax.ShapeDtypeStruct((t, A_W), BF16), jax.ShapeDtypeStruct((t, A_W), BF16),
            jax.ShapeDtypeStruct((t, A_W), BF16), jax.ShapeDtypeStruct((t, BQ_W), BF16),
            jax.ShapeDtypeStruct((t, 2 * BKV_W), BF16), jax.ShapeDtypeStruct((t, BKV_W), BF16),
            jax.ShapeDtypeStruct((tp, A_W), F32), jax.ShapeDtypeStruct((tp, A_W), F32),
            jax.ShapeDtypeStruct((tp, BKV_W), F32), jax.ShapeDtypeStruct((tp, BKV_W), F32),
        ],
        compiler_params=_cparams(1),
        name="attn_qkv",
    )(x, mod3, g_mix, w_in_bf, gq, gk, cos_t, sin_t, seg)


def _softmax_pv(qq, ks, vs):
    ss = [_dot_nt(qq, k) for k in ks]
    m = ss[0].max(axis=-1, keepdims=True)
    for s in ss[1:]:
        m = jnp.maximum(m, s.max(axis=-1, keepdims=True))
    acc = None
    l = None
    for s, v in zip(ss, vs):
        p = jnp.exp(s - m)
        ls = p.sum(axis=-1, keepdims=True)
        pv = _dot(p.astype(BF16), v)
        acc = pv if acc is None else acc + pv
        l = ls if l is None else l + ls
    return acc / l


def _split_halves(q):
    lo = _lane_iota(q.shape) < (LANES // 2)
    zero = jnp.zeros_like(q)
    return jnp.concatenate([jnp.where(lo, q, zero), jnp.where(lo, zero, q)], axis=0)


def _attn_body(lambda_init, qa, qb, ka_segs, va_segs, kbd_segs, vb_segs, lam, subln):
    tq = qa.shape[0]
    cols = []
    for h in range(A_HEADS):
        sl = slice(h * LANES, (h + 1) * LANES)
        o = _softmax_pv(_split_halves(qa[:, sl]), [k[:, sl] for k in ka_segs],
                        [v[:, sl] for v in va_segs])
        od = o[:tq] - lam * o[tq:]
        cols.append(_rms(od, subln) * (1.0 - lambda_init))
    lo = _lane_iota((tq, LANES)) < B_HEAD_DIM
    for c in range(BQ_W // LANES):
        g = (2 * c) // (B_Q_HEADS // B_KV_HEADS)
        sl = slice(c * LANES, (c + 1) * LANES)
        gl = slice(g * LANES, (g + 1) * LANES)
        o = _softmax_pv(_split_halves(qb[:, sl]), [k[:, gl] for k in kbd_segs], vb_segs)
        oe, oo = o[:tq], o[tq:]
        if g == 0:
            cols.append(jnp.where(lo, oe, pltpu.roll(oo, B_HEAD_DIM, 1)))
        else:
            cols.append(jnp.where(lo, pltpu.roll(oe, B_HEAD_DIM, 1), oo))
    return jnp.concatenate(cols, axis=1).astype(BF16)


def _lambda(lq1, lk1, lq2, lk2, lambda_init):
    return (jnp.exp(jnp.sum(lq1 * lk1, axis=-1, keepdims=True))
            - jnp.exp(jnp.sum(lq2 * lk2, axis=-1, keepdims=True)) + lambda_init)


def _attn_prompt_kernel(lambda_init, qa, ka, va, qb, kbd, vb, lq1, lk1, lq2, lk2, subln, o_ref):
    lam = _lambda(lq1[...], lk1[...], lq2[...], lk2[...], lambda_init)
    o_ref[...] = _attn_body(lambda_init, qa[...], qb[...], [ka[...]], [va[...]],
                            [kbd[...]], [vb[...]], lam, subln[...])


def _attn_sample_kernel(lambda_init, qa, ka, va, qb, kbd, vb, cka, cva, ckbd, cvb,
                        lq1, lk1, lq2, lk2, subln, o_ref):
    lam = _lambda(lq1[...], lk1[...], lq2[...], lk2[...], lambda_init)
    o_ref[...] = _attn_body(lambda_init, qa[...], qb[...], [ka[...], cka[...]], [va[...], cva[...]],
                            [kbd[...], ckbd[...]], [vb[...], cvb[...]], lam, subln[...])


def _attention(lambda_init, qa, ka, va, qb, kbd, vb, cka, cva, ckbd, cvb,
               lq1, lk1, lq2, lk2, subln, n_prompt, prompt_len, n_sample, sample_len):
    npt = n_prompt * prompt_len // TM
    small = [lq1, lk1, lq2, lk2, subln]
    widths = [A_W, A_W, A_W, BQ_W, 2 * BKV_W, BKV_W]
    assert prompt_len == TM
    full1 = lambda a: pl.BlockSpec(a.shape, lambda b: (0,) * a.ndim)
    o_p = pl.pallas_call(
        functools.partial(_attn_prompt_kernel, lambda_init),
        grid=(n_prompt,),
        in_specs=[pl.BlockSpec((TM, w), lambda b: (b, 0)) for w in widths] + [full1(a) for a in small],
        out_specs=pl.BlockSpec((TM, D_MODEL), lambda b: (b, 0)),
        out_shape=jax.ShapeDtypeStruct((npt * TM, D_MODEL), BF16),
        compiler_params=_cparams(1),
        name="attn_prompt",
    )(qa, ka, va, qb, kbd, vb, *small)
    tps = sample_len // TM
    seq0 = n_prompt * prompt_len // sample_len
    assert seq0 * sample_len == n_prompt * prompt_len
    past = cka.shape[1]
    qspec = lambda w: pl.BlockSpec((TM, w), lambda b, q: (npt + b * tps + q, 0))
    kspec = lambda w: pl.BlockSpec((sample_len, w), lambda b, q: (seq0 + b, 0))
    cspec = lambda w: pl.BlockSpec((None, past, w), lambda b, q: (b, 0, 0))
    full2 = lambda a: pl.BlockSpec(a.shape, lambda b, q: (0,) * a.ndim)
    o_s = pl.pallas_call(
        functools.partial(_attn_sample_kernel, lambda_init),
        grid=(n_sample, tps),
        in_specs=[qspec(A_W), kspec(A_W), kspec(A_W), qspec(BQ_W), kspec(2 * BKV_W), kspec(BKV_W),
                  cspec(A_W), cspec(A_W), cspec(2 * BKV_W), cspec(BKV_W)] + [full2(a) for a in small],
        out_specs=pl.BlockSpec((TM, D_MODEL), lambda b, q: (b * tps + q, 0)),
        out_shape=jax.ShapeDtypeStruct((n_sample * sample_len, D_MODEL), BF16),
        compiler_params=_cparams(2),
        name="attn_sample",
    )(qa, ka, va, qb, kbd, vb, cka, cva, ckbd, cvb, *small)
    return o_p, o_s


def _router_tail(i, x, y, mod, g_ffn, wr, br, xo_ref, h_ref, ri_ref, rg_ref, cnt_ref, carry_ref):
    gate_mix = mod[:, 2 * D_MODEL:3 * D_MODEL]
    shift = mod[:, 3 * D_MODEL:4 * D_MODEL]
    scale = mod[:, 4 * D_MODEL:5 * D_MODEL]
    xn = x + gate_mix * y
    xo_ref[...] = xn
    h = _rms(xn, g_ffn) * (1.0 + scale) + shift
    _store_token_tiles(h_ref, h)
    h_hi = h.astype(BF16)
    h_lo = (h - h_hi.astype(F32)).astype(BF16)
    w_hi = wr.astype(BF16)
    w_lo = (wr - w_hi.astype(F32)).astype(BF16)
    logits = _dot(h_hi, w_hi) + _dot(h_lo, w_hi) + _dot(h_hi, w_lo) + br
    lane = _lane_iota(logits.shape)
    lane_f = lane.astype(F32)
    vals, idxs = [], []
    l = logits
    for _ in range(TOP_K):
        m = l.max(axis=-1, keepdims=True)
        idx = jnp.where(l == m, lane_f, float(LANES)).min(axis=-1, keepdims=True)
        vals.append(m)
        idxs.append(idx)
        l = jnp.where(lane_f == idx, -jnp.inf, l)
    es = [jnp.exp(v - vals[0]) for v in vals]
    den = es[0]
    for e in es[1:]:
        den = den + e
    sel = jnp.zeros(logits.shape, F32)
    for idx in idxs:
        sel = sel + jnp.where(lane_f == idx, 1.0, 0.0)

    @pl.when(i == 0)
    def _():
        carry_ref[...] = jnp.zeros_like(carry_ref)

    carry = carry_ref[...]
    r_io = lax.broadcasted_iota(I32, (TM, TM), 0)
    c_io = lax.broadcasted_iota(I32, (TM, TM), 1)
    tri = jnp.where(c_io < r_io, 1.0, 0.0).astype(BF16)
    rank = _dot(tri, sel.astype(BF16)) + carry
    carry = carry + sel.sum(axis=0, keepdims=True)
    carry_ref[...] = carry
    cnt_ref[...] = carry
    ri = jnp.zeros(logits.shape, F32)
    rg = jnp.zeros(logits.shape, F32)
    for k in range(TOP_K):
        rk = jnp.where(lane_f == idxs[k], rank, 0.0).sum(axis=-1, keepdims=True)
        ri = jnp.where(lane == k, idxs[k], ri)
        ri = jnp.where(lane == TOP_K + k, rk, ri)
        rg = jnp.where(lane == k, es[k] / den, rg)
    ri_ref[...] = ri.astype(I32)
    rg_ref[...] = rg


def _tail_specs(t):
    row = lambda w: pl.BlockSpec((TM, w), lambda i: (i, 0))
    out_specs = [row(D_MODEL), pl.BlockSpec((TM * ROW_TILES, LANES), lambda i: (i, 0)), row(LANES), row(LANES),
                 pl.BlockSpec((1, LANES), lambda i: (0, 0))]
    out_shape = [jax.ShapeDtypeStruct((t, D_MODEL), F32), jax.ShapeDtypeStruct((t * ROW_TILES, LANES), F32),
                 jax.ShapeDtypeStruct((t, LANES), I32), jax.ShapeDtypeStruct((t, LANES), F32),
                 jax.ShapeDtypeStruct((1, LANES), F32)]
    return out_specs, out_shape


def _post_attn_kernel(npt, x_ref, op_ref, os_ref, mod_ref, w_ref, g_ref, wr_ref, br_ref,
                      xo_ref, h_ref, ri_ref, rg_ref, cnt_ref, carry_ref):
    i = pl.program_id(0)
    y = _dot(jnp.where(i < npt, op_ref[...], os_ref[...]), w_ref[...])
    _router_tail(i, x_ref[...], y, mod_ref[0], g_ref[...], wr_ref[...], br_ref[...],
                 xo_ref, h_ref, ri_ref, rg_ref, cnt_ref, carry_ref)


def _post_attn_call(x, o_p, o_s, mod3, layer, w_out_bf, g_ffn, wr, br, npt, tps):
    t = x.shape[0]
    row = lambda w: pl.BlockSpec((TM, w), lambda i: (i, 0))
    full = lambda a: pl.BlockSpec(a.shape, lambda i: (0,) * a.ndim)
    out_specs, out_shape = _tail_specs(t)
    return pl.pallas_call(
        functools.partial(_post_attn_kernel, npt),
        grid=(t // TM,),
        in_specs=[row(D_MODEL),
                  pl.BlockSpec((TM, D_MODEL), lambda i: (jnp.minimum(i, npt - 1), 0)),
                  pl.BlockSpec((TM, D_MODEL), lambda i: (jnp.maximum(i - npt, 0), 0)),
                  pl.BlockSpec((1, 1, N_MOD * D_MODEL), lambda i: (layer * 3 + _cond_row(i, npt, tps), 0, 0)),
                  full(w_out_bf), full(g_ffn), full(wr), full(br)],
        out_specs=out_specs, out_shape=out_shape,
        scratch_shapes=[pltpu.VMEM((1, LANES), F32)],
        compiler_params=_cparams(1),
        name="attn_out_router",
    )(x, o_p, o_s, mod3, w_out_bf, g_ffn, wr, br)


def _pool_kernel(npt, tps, x_ref, xp_ref, xn_ref, mod_ref, gm_ref, wp_ref, ps_ref, g_ref, wr_ref, br_ref,
                 xo_ref, h_ref, ri_ref, rg_ref, cnt_ref, carry_ref):
    i = pl.program_id(0)
    mod = mod_ref[0]
    shift = mod[:, 0:D_MODEL]
    scale = mod[:, D_MODEL:2 * D_MODEL]
    gm = gm_ref[...]
    x = x_ref[...]
    pre = lambda v: _rms(v, gm) * (1.0 + scale) + shift
    j = jnp.where(i < npt, 0, (i - npt) % tps)
    ntile = jnp.where(i < npt, 1, tps)
    has_prev = (j > 0).astype(F32)
    has_next = (j < ntile - 1).astype(F32)
    h = pre(x)
    hc = jnp.concatenate([pre(xp_ref[...]) * has_prev, h, pre(xn_ref[...]) * has_next], axis=0)
    rows = hc.shape[0]
    t_seq = (j * TM + lax.broadcasted_iota(I32, (TM, 1), 0)).astype(F32)
    seq_len = (ntile * TM).astype(F32)
    ys = []
    for g, w in enumerate(POOL_WINDOWS):
        half = w // 2
        s = hc[:, g * POOL_CH:(g + 1) * POOL_CH]
        step = 1
        while step < w:
            s = s + pltpu.roll(s, rows - step, 0)
            step *= 2
        s = pltpu.roll(s, half, 0) if half != HALO else s
        win = s[HALO:HALO + TM] if half != HALO else s[0:TM]
        cnt = jnp.minimum(t_seq + half, seq_len) - jnp.maximum(t_seq - half, 0.0)
        pooled = win / cnt - h[:, g * POOL_CH:(g + 1) * POOL_CH]
        ys.append(_dot(pooled.astype(BF16), wp_ref[g]))
    y = jnp.concatenate(ys, axis=1) * ps_ref[...]
    _router_tail(i, x, y, mod, g_ref[...], wr_ref[...], br_ref[...],
                 xo_ref, h_ref, ri_ref, rg_ref, cnt_ref, carry_ref)


def _pool_call(x, mod3, layer, g_mix, w_pool_bf, pool_scale, g_ffn, wr, br, npt, tps):
    t = x.shape[0]
    per = TM // HALO
    nh = t // HALO
    row = lambda w: pl.BlockSpec((TM, w), lambda i: (i, 0))
    full = lambda a: pl.BlockSpec(a.shape, lambda i: (0,) * a.ndim)
    out_specs, out_shape = _tail_specs(t)
    return pl.pallas_call(
        functools.partial(_pool_kernel, npt, tps),
        grid=(t // TM,),
        in_specs=[row(D_MODEL),
                  pl.BlockSpec((HALO, D_MODEL), lambda i: (jnp.maximum(i * per - 1, 0), 0)),
                  pl.BlockSpec((HALO, D_MODEL), lambda i: (jnp.minimum((i + 1) * per, nh - 1), 0)),
                  pl.BlockSpec((1, 1, N_MOD * D_MODEL), lambda i: (layer * 3 + _cond_row(i, npt, tps), 0, 0)),
                  full(g_mix), full(w_pool_bf), full(pool_scale), full(g_ffn), full(wr), full(br)],
        out_specs=out_specs, out_shape=out_shape,
        scratch_shapes=[pltpu.VMEM((1, LANES), F32)],
        compiler_params=_cparams(1),
        name="pool_router",
    )(x, x, x, mod3, g_mix, w_pool_bf, pool_scale, g_ffn, wr, br)


def _expert_rows(x, e, wgu_bf, wd_bf, bgu_ref, bd_ref, between=None):
    bgu = bgu_ref[pl.ds(e, 1), :]
    acc = jnp.zeros((TM, D_MODEL), F32)
    piece = 0
    for c in range(0, D_EXPERT, FF_CHUNK):
        gate = _dot(x, wgu_bf[:, c:c + FF_CHUNK]) + bgu[:, c:c + FF_CHUNK]
        up = _dot(x, wgu_bf[:, D_EXPERT + c:D_EXPERT + c + FF_CHUNK]) + bgu[:, D_EXPERT + c:D_EXPERT + c + FF_CHUNK]
        if between is not None:
            between(piece)
        gate = jnp.minimum(gate, SWIGLU_LIMIT)
        up = jnp.clip(up, -SWIGLU_LIMIT, SWIGLU_LIMIT)
        act = gate * jax.nn.sigmoid(SWIGLU_ALPHA * gate) * (up + 1.0)
        acc = acc + _dot(act.astype(BF16), wd_bf[c:c + FF_CHUNK, :])
        if between is not None:
            between(piece + 1)
        piece += 2
    return acc + bd_ref[pl.ds(e, 1), :]


EXPERT_PIECES = 2 * (D_EXPERT // FF_CHUNK)


ROW_GROUP = 8


def _for_rows(first_group, n, *fns):
    def group(g, c):
        for q in range(ROW_GROUP):
            for j, fn in enumerate(fns):
                fn(g * ROW_GROUP + q, (q + j) % 2)
        return c

    def single(r, c):
        for fn in fns:
            fn(r, 0)
        return c
    full = n // ROW_GROUP
    lax.fori_loop(first_group, full, group, 0)
    lax.fori_loop(jnp.maximum(full, first_group) * ROW_GROUP, n, single, 0)


def _gmm_kernel(layer, n_tok, pos_ref, te_ref, nv_ref, nxt_ref,
                h_hbm, wgu_hbm, wd_hbm, bgu_ref, bd_ref, yk_hbm,
                src_ref, cur_ref, xbuf, ybuf, wgu_st, wd_st, wgu_bf, wd_bf, sem_g, sem_s, sem_w):
    w = pl.program_id(0)
    n_steps = pl.num_programs(0)
    slot = w % 2
    nv = nv_ref[w]
    nv_next = jnp.where(w + 1 < n_steps, nv_ref[jnp.minimum(w + 1, n_steps - 1)], 0)

    def weights_copy(e):
        return (pltpu.make_async_copy(wgu_hbm.at[layer, e], wgu_st, sem_w.at[0]),
                pltpu.make_async_copy(wd_hbm.at[layer, e], wd_st, sem_w.at[1]))

    def tile_of(ref, start):
        if not isinstance(start, int):
            start = pl.multiple_of(start, ROW_TILES)
        return ref.at[pl.ds(start, ROW_TILES), :]

    def rows_of(ref, n):
        return ref.at[pl.ds(0, pl.multiple_of(n * ROW_TILES, ROW_TILES)), :]

    def gather_row(tile, dst_slot, n):
        def one(r, prio):
            a8 = src_ref[tile * TM + jnp.minimum(r, n - 1)]
            tok_row = a8 & (n_tok * ROW_TILES - 1)
            pltpu.make_async_copy(tile_of(h_hbm, tok_row), tile_of(xbuf.at[dst_slot], r * ROW_TILES),
                                  sem_g.at[dst_slot]).start(priority=prio)
        return one

    def gather_wait(dst_slot):
        pltpu.make_async_copy(rows_of(h_hbm, TM), xbuf.at[dst_slot], sem_g.at[dst_slot]).wait()

    def scatter_row(tile, src_slot):
        def one(r, prio):
            pltpu.make_async_copy(tile_of(ybuf.at[src_slot], r * ROW_TILES),
                                  tile_of(yk_hbm, src_ref[tile * TM + r]),
                                  sem_s.at[src_slot]).start(priority=prio)
        return one

    def scatter_wait(src_slot, n):
        pltpu.make_async_copy(rows_of(ybuf.at[src_slot], n), rows_of(yk_hbm, n), sem_s.at[src_slot]).wait()

    @pl.when(w == 0)
    def _():
        def invert(a, c):
            src_ref[pos_ref[a]] = a * ROW_TILES
            return c
        lax.fori_loop(0, n_tok * TOP_K, invert, 0, unroll=8)
        cur_ref[0] = -1
        for cp in weights_copy(te_ref[0]):
            cp.start()
        _for_rows(0, TM, gather_row(0, 0, nv))

    @pl.when(nv > 0)
    def _():
        e = te_ref[w]
        nv_prev = jnp.where(w >= 1, nv_ref[jnp.maximum(w - 1, 0)], 0)
        gather_wait(slot)

        @pl.when(w >= 2)
        def _():
            scatter_wait(slot, nv_ref[jnp.maximum(w - 2, 0)])

        gather_next = gather_row(jnp.where(nv_next > 0, w + 1, w), 1 - slot, jnp.where(nv_next > 0, nv_next, nv))
        scatter_prev = scatter_row(w - 1, 1 - slot)

        @pl.when(cur_ref[0] != e)
        def _():
            for cp in weights_copy(e):
                cp.wait()
            wgu_bf[...] = wgu_st[...].astype(BF16)
            wd_bf[...] = wd_st[...].astype(BF16)
            cur_ref[0] = e

            @pl.when(nxt_ref[e] < N_EXPERTS)
            def _():
                for cp in weights_copy(nxt_ref[e]):
                    cp.start()

        x = _load_token_tiles(xbuf.at[slot], TM).astype(BF16)

        def run(scatter_inline):
            def between(piece):
                rows = TM // EXPERT_PIECES
                for r in range(piece * rows, (piece + 1) * rows):
                    gather_next(r, r % 2)
                    if scatter_inline:
                        scatter_prev(r, (r + 1) % 2)
            _store_token_tiles(ybuf.at[slot], _expert_rows(x, e, wgu_bf, wd_bf, bgu_ref, bd_ref, between))

        @pl.when(nv_prev == TM)
        def _():
            run(True)

        @pl.when(nv_prev != TM)
        def _():
            _for_rows(0, nv_prev, scatter_prev)
            run(False)

        @pl.when(nv_next == 0)
        def _():
            _for_rows(0, nv, scatter_row(w, slot))

            @pl.when(w >= 1)
            def _():
                scatter_wait(1 - slot, nv_prev)
            scatter_wait(slot, nv)
            gather_wait(1 - slot)


def _gmm_call(pos, te, nv, nxt, h, w_gu, w_dn, b_gu, b_dn, layer):
    assert ROW_TILES == SUBLANES
    t = h.shape[0] // ROW_TILES
    assert t & (t - 1) == 0
    n_rows = t * TOP_K
    n_steps = te.shape[0]
    return pl.pallas_call(
        functools.partial(_gmm_kernel, layer, t),
        grid_spec=pltpu.PrefetchScalarGridSpec(
            num_scalar_prefetch=4, grid=(n_steps,),
            in_specs=[pl.BlockSpec(memory_space=pl.ANY), pl.BlockSpec(memory_space=pl.ANY),
                      pl.BlockSpec(memory_space=pl.ANY),
                      pl.BlockSpec((None, N_EXPERTS, 2 * D_EXPERT), lambda w, *_: (layer, 0, 0)),
                      pl.BlockSpec((None, N_EXPERTS, D_MODEL), lambda w, *_: (layer, 0, 0))],
            out_specs=pl.BlockSpec(memory_space=pl.ANY),
            scratch_shapes=[
                pltpu.SMEM((n_steps * TM,), I32), pltpu.SMEM((1,), I32),
                pltpu.VMEM((2, TM * ROW_TILES, LANES), F32), pltpu.VMEM((2, TM * ROW_TILES, LANES), F32),
                pltpu.VMEM((D_MODEL, 2 * D_EXPERT), F32), pltpu.VMEM((D_EXPERT, D_MODEL), F32),
                pltpu.VMEM((D_MODEL, 2 * D_EXPERT), BF16), pltpu.VMEM((D_EXPERT, D_MODEL), BF16),
                pltpu.SemaphoreType.DMA((2,)), pltpu.SemaphoreType.DMA((2,)), pltpu.SemaphoreType.DMA((2,))]),
        out_shape=jax.ShapeDtypeStruct((n_rows * ROW_TILES, LANES), F32),
        compiler_params=_cparams(1),
        name="moe_experts",
    )(pos, te, nv, nxt, h, w_gu, w_dn, b_gu, b_dn)


def _combine_kernel(final, npt, x_ref, rg_ref, mod_ref, gf_ref, y0_ref, y1_ref, y2_ref, y3_ref, *outs):
    i = pl.program_id(0)
    rg = rg_ref[...]
    moe = rg[:, 0:1] * _load_token_tiles(y0_ref, TM)
    for k, y_ref in enumerate((y1_ref, y2_ref, y3_ref), start=1):
        moe = moe + rg[:, k:k + 1] * _load_token_tiles(y_ref, TM)
    gate_ffn = mod_ref[0][:, 5 * D_MODEL:6 * D_MODEL]
    xn = x_ref[...] + gate_ffn * moe
    if not final:
        outs[0][...] = xn
        return
    y = _rms(xn, gf_ref[...])
    yp_ref, ys_ref = outs

    @pl.when(i < npt)
    def _():
        yp_ref[...] = y

    @pl.when(i >= npt)
    def _():
        ys_ref[...] = y


def _combine_call(yk, x, rg, mod3, layer, g_final, final, npt, tps):
    t = x.shape[0]
    nt = t // TM
    assert TOP_K == 4
    row = lambda w: pl.BlockSpec((TM, w), lambda i: (i, 0))
    slab = lambda k: pl.BlockSpec((TM * ROW_TILES, LANES), lambda i: (k * nt + i, 0))
    if final:
        out_specs = [pl.BlockSpec((TM, D_MODEL), lambda i: (jnp.minimum(i, npt - 1), 0)),
                     pl.BlockSpec((TM, D_MODEL), lambda i: (jnp.maximum(i - npt, 0), 0))]
        out_shape = [jax.ShapeDtypeStruct((npt * TM, D_MODEL), F32),
                     jax.ShapeDtypeStruct(((nt - npt) * TM, D_MODEL), F32)]
    else:
        out_specs = [row(D_MODEL)]
        out_shape = [jax.ShapeDtypeStruct((t, D_MODEL), F32)]
    return pl.pallas_call(
        functools.partial(_combine_kernel, final, npt),
        grid=(nt,),
        in_specs=[row(D_MODEL), row(LANES),
                  pl.BlockSpec((1, 1, N_MOD * D_MODEL), lambda i: (layer * 3 + _cond_row(i, npt, tps), 0, 0)),
                  pl.BlockSpec(g_final.shape, lambda i: (0, 0))] + [slab(k) for k in range(TOP_K)],
        out_specs=out_specs, out_shape=out_shape,
        compiler_params=_cparams(1),
        name="moe_combine",
    )(x, rg, mod3, g_final, yk, yk, yk, yk)


def _routing_tables(ri, cnt, n_tok):
    ex = jnp.arange(N_EXPERTS, dtype=I32)
    counts = cnt[0, :N_EXPERTS].astype(I32)
    tiles = (counts + (TM - 1)) // TM
    tile_end = jnp.sum(jnp.where(ex[None, :] <= ex[:, None], tiles[None, :], 0), axis=1)
    tile_start = tile_end - tiles
    e_idx = ri[:, :TOP_K]
    rank = ri[:, TOP_K:2 * TOP_K]
    onehot = e_idx[:, :, None] == ex[None, None, :]
    pos = rank + jnp.sum(jnp.where(onehot, tile_start[None, None, :] * TM, 0), axis=-1)
    pos = pos.T.reshape(-1).astype(I32)
    n_steps = n_tok * TOP_K // TM + N_EXPERTS
    w = jnp.arange(n_steps, dtype=I32)
    te = jnp.minimum(jnp.sum((tile_end[None, :] <= w[:, None]).astype(I32), axis=1), N_EXPERTS - 1)
    mine = te[:, None] == ex[None, :]
    left = jnp.sum(jnp.where(mine, (counts - (w[:, None] - tile_start[None, :]) * TM)[...], 0), axis=1)
    nv = jnp.where(w < tile_end[-1], jnp.clip(left, 0, TM), 0)
    later = (ex[None, :] > ex[:, None]) & (counts[None, :] > 0)
    nxt = jnp.min(jnp.where(later, ex[None, :], N_EXPERTS), axis=1)
    return pos, te.astype(I32), nv.astype(I32), nxt.astype(I32)


def _moe(x, h, ri, rg, cnt, mod3, layer, w_gu, b_gu, w_dn, b_dn, g_final, final, npt, tps):
    pos, te, nv, nxt = _routing_tables(ri, cnt, x.shape[0])
    yk = _gmm_call(pos, te, nv, nxt, h, w_gu, w_dn, b_gu, b_dn, layer)
    return _combine_call(yk, x, rg, mod3, layer, g_final, final, npt, tps)


def _rope_tables(n_tokens):
    n_rows = n_tokens // GRID_W
    rows = jnp.repeat(jnp.arange(n_rows, dtype=F32), GRID_W)
    cols = jnp.tile(jnp.arange(GRID_W, dtype=F32), n_rows)
    axis_dim = A_HEAD_DIM // 2
    inv = ROPE_THETA ** (-jnp.arange(0, axis_dim, 2, dtype=F32) / axis_dim)
    ang_r = rows[:, None] * inv[None, :]
    ang_c = cols[:, None] * inv[None, :]
    ang = jnp.concatenate([ang_r, ang_r, ang_c, ang_c], axis=-1)
    cos, sin = jnp.cos(ang), jnp.sin(ang)
    sign = jnp.where((jnp.arange(A_HEAD_DIM) % 32) < 16, -1.0, 1.0).astype(F32)
    sin = sin * sign[None, :]
    cos = jnp.concatenate([jnp.ones((TM, A_HEAD_DIM), F32), cos], axis=0)
    sin = jnp.concatenate([jnp.zeros((TM, A_HEAD_DIM), F32), sin], axis=0)
    return jnp.tile(cos, (1, LANES // A_HEAD_DIM)), jnp.tile(sin, (1, LANES // A_HEAD_DIM))


def kernel(x_prompt, x_sample, cache_diff_k, cache_diff_v, cache_gqa_k, cache_gqa_v, c, c_ctx, w_ada, b_ada, norm_mix, norm_ffn, norm_final, w_attn_in, w_attn_out, lam_q1, lam_k1, lam_q2, lam_k2, diff_subln, gqa_q_norm, gqa_k_norm, w_pool, pool_scale, w_router, b_router, w_gate_up, b_gate_up, w_down, b_down):
    n_prompt, prompt_len, d = x_prompt.shape
    n_sample, sample_len, _ = x_sample.shape
    assert d == D_MODEL and prompt_len % TM == 0 and sample_len % TM == 0
    assert n_sample + 1 <= SUBLANES
    npt = n_prompt * prompt_len // TM
    tps = sample_len // TM
    tp = n_prompt * prompt_len
    past = cache_diff_k.shape[2]

    x = jnp.concatenate([x_prompt.reshape(tp, d), x_sample.reshape(n_sample * sample_len, d)], axis=0)
    cond8 = jnp.zeros((SUBLANES, d), F32).at[0].set(c_ctx).at[1:1 + n_sample].set(c)
    mod = _modulation_all(cond8, w_ada, b_ada)
    mod3 = mod[:, :1 + n_sample].reshape(DEPTH * (1 + n_sample), 1, N_MOD * d)
    assert n_sample == 2

    cos_t, sin_t = _rope_tables(sample_len)
    seg_r = jnp.arange(BQ_W)[:, None] // B_HEAD_DIM
    seg = (seg_r == seg_r.T).astype(BF16)

    caches = []
    y_final = None
    for i in range(DEPTH):
        j = i // 2
        g_mix = norm_mix[i][None]
        g_ffn = norm_ffn[i][None]
        wr = jnp.zeros((d, LANES), F32).at[:, :N_EXPERTS].set(w_router[i])
        br = jnp.full((1, LANES), NEG_BIG, F32).at[0, :N_EXPERTS].set(b_router[i])
        if i % 2 == 0:
            lambda_init = 0.8 - 0.6 * math.exp(-0.3 * i)
            gq = jnp.tile(gqa_q_norm[j], BQ_W // B_HEAD_DIM)[None]
            gk = jnp.tile(gqa_k_norm[j], BKV_W // B_HEAD_DIM)[None]
            qa, ka, va, qb, kbd, vb, ck, cv, cgk, cgv = _qkv_call(
                x, mod3, i, g_mix, w_attn_in[j].astype(BF16), gq, gk, cos_t, sin_t, seg, npt, tps)
            caches.append((ck, cv, cgk, cgv))
            cka = cache_diff_k[:, j].reshape(n_sample, past, A_W).astype(BF16)
            cva = cache_diff_v[:, j].reshape(n_sample, past, A_W).astype(BF16)
            gk_c = cache_gqa_k[:, j]
            ckbd = jnp.concatenate([gk_c[:, :, 0], gk_c[:, :, 0], gk_c[:, :, 1], gk_c[:, :, 1]],
                                   axis=-1).astype(BF16)
            cvb = cache_gqa_v[:, j].reshape(n_sample, past, BKV_W).astype(BF16)
            o_p, o_s = _attention(lambda_init, qa, ka, va, qb, kbd, vb, cka, cva, ckbd, cvb,
                                  lam_q1[j][None], lam_k1[j][None], lam_q2[j][None], lam_k2[j][None],
                                  diff_subln[j][None], n_prompt, prompt_len, n_sample, sample_len)
            x, h, ri, rg, cnt = _post_attn_call(x, o_p, o_s, mod3, i, w_attn_out[j].astype(BF16), g_ffn, wr, br,
                                                npt, tps)
        else:
            x, h, ri, rg, cnt = _pool_call(x, mod3, i, g_mix, w_pool[j].astype(BF16), pool_scale[j][None],
                                           g_ffn, wr, br, npt, tps)
        final = i == DEPTH - 1
        outs = _moe(x, h, ri, rg, cnt, mod3, i, w_gate_up, b_gate_up, w_down, b_down, norm_final[None], final, npt, tps)
        if final:
            y_final = outs
        else:
            x = outs[0]

    y_prompt = y_final[0].reshape(n_prompt, prompt_len, d)
    y_sample = y_final[1].reshape(n_sample, sample_len, d)
    stack = lambda k, shp: jnp.stack([cc[k].reshape(shp) for cc in caches], axis=1)
    new_diff_k = stack(0, (n_prompt, prompt_len, A_HEADS, 2 * A_HEAD_DIM))
    new_diff_v = stack(1, (n_prompt, prompt_len, A_HEADS, 2 * A_HEAD_DIM))
    new_gqa_k = stack(2, (n_prompt, prompt_len, B_KV_HEADS, B_HEAD_DIM))
    new_gqa_v = stack(3, (n_prompt, prompt_len, B_KV_HEADS, B_HEAD_DIM))
    return (y_prompt, y_sample, new_diff_k, new_diff_v, new_gqa_k, new_gqa_v)
```

```python
import functools
import math

import jax
import jax.numpy as jnp
from jax import lax
from jax.experimental import pallas as pl
from jax.experimental.pallas import tpu as pltpu

F32 = jnp.float32
BF16 = jnp.bfloat16
I32 = jnp.int32
U32 = jnp.uint32

D_MODEL = 1024
DEPTH = 4
GRID_W = 64
A_HEADS = 4
A_HEAD_DIM = 64
B_Q_HEADS = 8
B_KV_HEADS = 2
B_HEAD_DIM = 64
ROPE_THETA = 10000.0
A_W = A_HEADS * 2 * A_HEAD_DIM
BQ_W = B_Q_HEADS * B_HEAD_DIM
BKV_W = B_KV_HEADS * B_HEAD_DIM
IN_W = 3 * A_W + BQ_W + 2 * BKV_W
POOL_WINDOWS = (2, 4, 8, 16)
POOL_GROUPS = 4
POOL_CH = D_MODEL // POOL_GROUPS
N_EXPERTS = 32
TOP_K = 4
D_EXPERT = D_MODEL
SWIGLU_LIMIT = 7.0
SWIGLU_ALPHA = 1.702
N_MOD = 6
EPS = 1e-6

LANES = 128
SUBLANES = 8
TM = 256
ROW_TILES = D_MODEL // LANES
HALO = 8
ADA_TN = 1536
FF_CHUNK = 512
VMEM_LIMIT = 56 * 1024 * 1024
NEG_BIG = -1e30


def _cparams(n_axes, vmem=VMEM_LIMIT):
    return pltpu.CompilerParams(
        dimension_semantics=("arbitrary",) * n_axes, vmem_limit_bytes=vmem)


def _dot(a, b):
    return jnp.dot(a, b, preferred_element_type=F32)


def _dot_nt(a, b):
    return lax.dot_general(a, b, (((1,), (1,)), ((), ())), preferred_element_type=F32)


def _rms(x, g):
    ms = jnp.mean(x * x, axis=-1, keepdims=True)
    return x * lax.rsqrt(ms + EPS) * g


def _lane_iota(shape):
    return lax.broadcasted_iota(I32, shape, len(shape) - 1)


def _store_token_tiles(ref, x, mask=None):
    n = x.shape[0]
    for c in range(ROW_TILES):
        idx = (pl.ds(c, n, stride=ROW_TILES), slice(None))
        v = x[:, c * LANES:(c + 1) * LANES]
        ref[idx] = v if mask is None else jnp.where(mask, v, ref[idx])


def _load_token_tiles(ref, n):
    return jnp.concatenate([ref[pl.ds(c, n, stride=ROW_TILES), :] for c in range(ROW_TILES)], axis=1)


def _ada_kernel(c_ref, w_ref, b_ref, o_ref):
    c = c_ref[...]
    s = (c * jax.nn.sigmoid(c)).astype(BF16)
    o_ref[0] = _dot(s, w_ref[0].astype(BF16)) + b_ref[0]


def _modulation_all(cond8, w_ada, b_ada):
    nmod = w_ada.shape[-1]
    return pl.pallas_call(
        _ada_kernel,
        grid=(DEPTH, nmod // ADA_TN),
        in_specs=[
            pl.BlockSpec((SUBLANES, D_MODEL), lambda l, n: (0, 0)),
            pl.BlockSpec((1, D_MODEL, ADA_TN), lambda l, n: (l, 0, n)),
            pl.BlockSpec((1, 1, ADA_TN), lambda l, n: (l, 0, n)),
        ],
        out_specs=pl.BlockSpec((1, SUBLANES, ADA_TN), lambda l, n: (l, 0, n)),
        out_shape=jax.ShapeDtypeStruct((DEPTH, SUBLANES, nmod), F32),
        compiler_params=_cparams(2),
        name="modulation",
    )(cond8, w_ada, b_ada.reshape(DEPTH, 1, nmod))


def _seg_meansq(x, seg):
    x2 = x * x
    hi = x2.astype(BF16)
    lo = (x2 - hi.astype(F32)).astype(BF16)
    return (_dot(hi, seg) + _dot(lo, seg)) * (1.0 / B_HEAD_DIM)


def _rope128(x, cos, sin_signed):
    lo16 = (_lane_iota(x.shape) % 32) < 16
    nxt = pltpu.roll(x, LANES - 16, 1)
    prv = pltpu.roll(x, 16, 1)
    return x * cos + jnp.where(lo16, nxt, prv) * sin_signed


def _rope(x, cos, sin_signed):
    cols = [_rope128(x[:, c:c + LANES], cos, sin_signed) for c in range(0, x.shape[1], LANES)]
    return cols[0] if len(cols) == 1 else jnp.concatenate(cols, axis=1)


def _row_specs(xs, npt):
    if len(xs) == 1:
        return [pl.BlockSpec((TM, D_MODEL), lambda i: (i, 0))]
    return [pl.BlockSpec((TM, D_MODEL), lambda i: (jnp.minimum(i, npt - 1), 0)),
            pl.BlockSpec((TM, D_MODEL), lambda i: (jnp.maximum(i - npt, 0), 0))]


def _read_rows(i, npt, x_refs):
    if len(x_refs) == 1:
        return x_refs[0][...]
    return jnp.where(i < npt, x_refs[0][...], x_refs[1][...])


def _qkv_kernel(n_prompt_tiles, n_x, *refs):
    x_refs = refs[:n_x]
    (mod_ref, g_ref, w_ref, gq_ref, gk_ref, cos_ref, sin_ref, seg_ref,
     qa_ref, ka_ref, va_ref, qb_ref, kbd_ref, vb_ref, ck_ref, cv_ref, cgk_ref, cgv_ref) = refs[n_x:]
    i = pl.program_id(0)
    mod = mod_ref[0]
    shift = mod[:, 0:D_MODEL]
    scale = mod[:, D_MODEL:2 * D_MODEL]
    h = _rms(_read_rows(i, n_prompt_tiles, x_refs), g_ref[...]) * (1.0 + scale) + shift
    p = _dot(h.astype(BF16), w_ref[...])
    a_q = p[:, 0:A_W]
    a_k = p[:, A_W:2 * A_W]
    a_v = p[:, 2 * A_W:3 * A_W]
    o = 3 * A_W
    b_q = p[:, o:o + BQ_W]
    b_k = p[:, o + BQ_W:o + BQ_W + BKV_W]
    b_v = p[:, o + BQ_W + BKV_W:o + BQ_W + 2 * BKV_W]
    seg = seg_ref[...]
    b_q = b_q * lax.rsqrt(_seg_meansq(b_q, seg) + EPS) * gq_ref[...]
    b_k = b_k * lax.rsqrt(_seg_meansq(b_k, seg[:BKV_W, :BKV_W]) + EPS) * gk_ref[...]

    @pl.when(i < n_prompt_tiles)
    def _():
        ck_ref[...] = a_k
        cv_ref[...] = a_v
        cgk_ref[...] = b_k
        cgv_ref[...] = b_v

    cos = cos_ref[...]
    sin = sin_ref[...]
    sm = A_HEAD_DIM ** -0.5
    qa_ref[...] = (_rope(a_q, cos, sin) * sm).astype(BF16)
    ka_ref[...] = _rope(a_k, cos, sin).astype(BF16)
    va_ref[...] = a_v.astype(BF16)
    qb_ref[...] = (_rope(b_q, cos, sin) * (B_HEAD_DIM ** -0.5)).astype(BF16)
    kb = _rope(b_k, cos, sin)
    kb_sw = pltpu.roll(kb, B_HEAD_DIM, 1)
    lo = _lane_iota(kb.shape) < B_HEAD_DIM
    kbd_ref[...] = jnp.concatenate(
        [jnp.where(lo, kb, kb_sw), jnp.where(lo, kb_sw, kb)], axis=1).astype(BF16)
    vb_ref[...] = b_v.astype(BF16)


def _cond_row(i, npt, tiles_per_seq):
    return jnp.where(i < npt, 0, 1 + (i - npt) // tiles_per_seq)


def _qkv_call(xs, mod3, layer, g_mix, w_in_bf, gq, gk, cos_t, sin_t, seg, npt, tps):
    t = sum(x.shape[0] for x in xs)
    nt = t // TM
    tp = npt * TM
    row = lambda w: pl.BlockSpec((TM, w), lambda i: (i, 0))
    full = lambda a: pl.BlockSpec(a.shape, lambda i: (0,) * a.ndim)
    tab = pl.BlockSpec((TM, LANES), lambda i: (jnp.where(i < npt, 0, 1 + (i - npt) % tps), 0))
    cache = lambda w: pl.BlockSpec((TM, w), lambda i: (jnp.minimum(i, npt - 1), 0))
    return pl.pallas_call(
        functools.partial(_qkv_kernel, npt, len(xs)),
        grid=(nt,),
        in_specs=_row_specs(xs, npt) + [
            pl.BlockSpec((1, 1, N_MOD * D_MODEL), lambda i: (layer * 3 + _cond_row(i, npt, tps), 0, 0)),
            full(g_mix), full(w_in_bf), full(gq), full(gk), tab, tab, full(seg),
        ],
        out_specs=[row(A_W), row(A_W), row(A_W), row(BQ_W), row(2 * BKV_W), row(BKV_W),
                   cache(A_W), cache(A_W), cache(BKV_W), cache(BKV_W)],
        out_shape=[
            jax.ShapeDtypeStruct((t, A_W), BF16), jax.ShapeDtypeStruct((t, A_W), BF16),
            jax.ShapeDtypeStruct((t, A_W), BF16), jax.ShapeDtypeStruct((t, BQ_W), BF16),
            jax.ShapeDtypeStruct((t, 2 * BKV_W), BF16), jax.ShapeDtypeStruct((t, BKV_W), BF16),
            jax.ShapeDtypeStruct((tp, A_W), F32), jax.ShapeDtypeStruct((tp, A_W), F32),
            jax.ShapeDtypeStruct((tp, BKV_W), F32), jax.ShapeDtypeStruct((tp, BKV_W), F32),
        ],
        compiler_params=_cparams(1),
        name="attn_qkv",
    )(*xs, mod3, g_mix, w_in_bf, gq, gk, cos_t, sin_t, seg)


def _softmax_pv(qq, ks, vs):
    ss = [_dot_nt(qq, k) for k in ks]
    m = ss[0].max(axis=-1, keepdims=True)
    for s in ss[1:]:
        m = jnp.maximum(m, s.max(axis=-1, keepdims=True))
    acc = None
    l = None
    for s, v in zip(ss, vs):
        p = jnp.exp(s - m)
        ls = p.sum(axis=-1, keepdims=True)
        pv = _dot(p.astype(BF16), v)
        acc = pv if acc is None else acc + pv
        l = ls if l is None else l + ls
    return acc / l


def _split_halves(q):
    lo = _lane_iota(q.shape) < (LANES // 2)
    zero = jnp.zeros_like(q)
    return jnp.concatenate([jnp.where(lo, q, zero), jnp.where(lo, zero, q)], axis=0)


def _attn_body(lambda_init, qa, qb, ka_segs, va_segs, kbd_segs, vb_segs, lam, subln):
    tq = qa.shape[0]
    cols = []
    for h in range(A_HEADS):
        sl = slice(h * LANES, (h + 1) * LANES)
        o = _softmax_pv(_split_halves(qa[:, sl]), [k[:, sl] for k in ka_segs],
                        [v[:, sl] for v in va_segs])
        od = o[:tq] - lam * o[tq:]
        cols.append(_rms(od, subln) * (1.0 - lambda_init))
    lo = _lane_iota((tq, LANES)) < B_HEAD_DIM
    for c in range(BQ_W // LANES):
        g = (2 * c) // (B_Q_HEADS // B_KV_HEADS)
        sl = slice(c * LANES, (c + 1) * LANES)
        gl = slice(g * LANES, (g + 1) * LANES)
        o = _softmax_pv(_split_halves(qb[:, sl]), [k[:, gl] for k in kbd_segs], vb_segs)
        oe, oo = o[:tq], o[tq:]
        if g == 0:
            cols.append(jnp.where(lo, oe, pltpu.roll(oo, B_HEAD_DIM, 1)))
        else:
            cols.append(jnp.where(lo, pltpu.roll(oe, B_HEAD_DIM, 1), oo))
    return jnp.concatenate(cols, axis=1).astype(BF16)


def _lambda(lq1, lk1, lq2, lk2, lambda_init):
    return (jnp.exp(jnp.sum(lq1 * lk1, axis=-1, keepdims=True))
            - jnp.exp(jnp.sum(lq2 * lk2, axis=-1, keepdims=True)) + lambda_init)


def _attn_prompt_kernel(lambda_init, qa, ka, va, qb, kbd, vb, lq1, lk1, lq2, lk2, subln, o_ref):
    lam = _lambda(lq1[...], lk1[...], lq2[...], lk2[...], lambda_init)
    o_ref[...] = _attn_body(lambda_init, qa[...], qb[...], [ka[...]], [va[...]],
                            [kbd[...]], [vb[...]], lam, subln[...])


def _attn_sample_kernel(lambda_init, qa, ka, va, qb, kbd, vb, cka, cva, ckbd, cvb,
                        lq1, lk1, lq2, lk2, subln, o_ref):
    lam = _lambda(lq1[...], lk1[...], lq2[...], lk2[...], lambda_init)
    o_ref[...] = _attn_body(lambda_init, qa[...], qb[...], [ka[...], cka[...]], [va[...], cva[...]],
                            [kbd[...], ckbd[...]], [vb[...], cvb[...]], lam, subln[...])


def _attention(lambda_init, qa, ka, va, qb, kbd, vb, cka, cva, ckbd, cvb,
               lq1, lk1, lq2, lk2, subln, n_prompt, prompt_len, n_sample, sample_len):
    npt = n_prompt * prompt_len // TM
    small = [lq1, lk1, lq2, lk2, subln]
    widths = [A_W, A_W, A_W, BQ_W, 2 * BKV_W, BKV_W]
    assert prompt_len == TM
    full1 = lambda a: pl.BlockSpec(a.shape, lambda b: (0,) * a.ndim)
    o_p = pl.pallas_call(
        functools.partial(_attn_prompt_kernel, lambda_init),
        grid=(n_prompt,),
        in_specs=[pl.BlockSpec((TM, w), lambda b: (b, 0)) for w in widths] + [full1(a) for a in small],
        out_specs=pl.BlockSpec((TM, D_MODEL), lambda b: (b, 0)),
        out_shape=jax.ShapeDtypeStruct((npt * TM, D_MODEL), BF16),
        compiler_params=_cparams(1),
        name="attn_prompt",
    )(qa, ka, va, qb, kbd, vb, *small)
    tps = sample_len // TM
    seq0 = n_prompt * prompt_len // sample_len
    assert seq0 * sample_len == n_prompt * prompt_len
    past = cka.shape[1]
    qspec = lambda w: pl.BlockSpec((TM, w), lambda b, q: (npt + b * tps + q, 0))
    kspec = lambda w: pl.BlockSpec((sample_len, w), lambda b, q: (seq0 + b, 0))
    cspec = lambda w: pl.BlockSpec((None, past, w), lambda b, q: (b, 0, 0))
    full2 = lambda a: pl.BlockSpec(a.shape, lambda b, q: (0,) * a.ndim)
    o_s = pl.pallas_call(
        functools.partial(_attn_sample_kernel, lambda_init),
        grid=(n_sample, tps),
        in_specs=[qspec(A_W), kspec(A_W), kspec(A_W), qspec(BQ_W), kspec(2 * BKV_W), kspec(BKV_W),
                  cspec(A_W), cspec(A_W), cspec(2 * BKV_W), cspec(BKV_W)] + [full2(a) for a in small],
        out_specs=pl.BlockSpec((TM, D_MODEL), lambda b, q: (b * tps + q, 0)),
        out_shape=jax.ShapeDtypeStruct((n_sample * sample_len, D_MODEL), BF16),
        compiler_params=_cparams(2),
        name="attn_sample",
    )(qa, ka, va, qb, kbd, vb, cka, cva, ckbd, cvb, *small)
    return o_p, o_s


def _router_tail(i, x, y, mod, g_ffn, wr, br, xo_ref, h_ref, ri_ref, rg_ref, cnt_ref, carry_ref):
    gate_mix = mod[:, 2 * D_MODEL:3 * D_MODEL]
    shift = mod[:, 3 * D_MODEL:4 * D_MODEL]
    scale = mod[:, 4 * D_MODEL:5 * D_MODEL]
    xn = x + gate_mix * y
    xo_ref[...] = xn
    h = _rms(xn, g_ffn) * (1.0 + scale) + shift
    _store_token_tiles(h_ref, h)
    h_hi = h.astype(BF16)
    h_lo = (h - h_hi.astype(F32)).astype(BF16)
    w_hi = wr.astype(BF16)
    w_lo = (wr - w_hi.astype(F32)).astype(BF16)
    logits = _dot(h_hi, w_hi) + _dot(h_lo, w_hi) + _dot(h_hi, w_lo) + br
    lane = _lane_iota(logits.shape)
    lane_f = lane.astype(F32)
    vals, idxs = [], []
    l = logits
    for _ in range(TOP_K):
        m = l.max(axis=-1, keepdims=True)
        idx = jnp.where(l == m, lane_f, float(LANES)).min(axis=-1, keepdims=True)
        vals.append(m)
        idxs.append(idx)
        l = jnp.where(lane_f == idx, -jnp.inf, l)
    es = [jnp.exp(v - vals[0]) for v in vals]
    den = es[0]
    for e in es[1:]:
        den = den + e
    sel = jnp.zeros(logits.shape, F32)
    for idx in idxs:
        sel = sel + jnp.where(lane_f == idx, 1.0, 0.0)

    @pl.when(i == 0)
    def _():
        carry_ref[...] = jnp.zeros_like(carry_ref)

    carry = carry_ref[...]
    r_io = lax.broadcasted_iota(I32, (TM, TM), 0)
    c_io = lax.broadcasted_iota(I32, (TM, TM), 1)
    tri = jnp.where(c_io < r_io, 1.0, 0.0).astype(BF16)
    rank = _dot(tri, sel.astype(BF16)) + carry
    carry = carry + sel.sum(axis=0, keepdims=True)
    carry_ref[...] = carry
    cnt_ref[...] = carry
    ri = jnp.zeros(logits.shape, F32)
    rg = jnp.zeros(logits.shape, F32)
    for k in range(TOP_K):
        rk = jnp.where(lane_f == idxs[k], rank, 0.0).sum(axis=-1, keepdims=True)
        ri = jnp.where(lane == k, idxs[k], ri)
        ri = jnp.where(lane == TOP_K + k, rk, ri)
        rg = jnp.where(lane == k, es[k] / den, rg)
    ri_ref[...] = ri.astype(I32)
    rg_ref[...] = rg


def _tail_specs(t):
    row = lambda w: pl.BlockSpec((TM, w), lambda i: (i, 0))
    out_specs = [row(D_MODEL), pl.BlockSpec((TM * ROW_TILES, LANES), lambda i: (i, 0)), row(LANES), row(LANES),
                 pl.BlockSpec((1, LANES), lambda i: (0, 0))]
    out_shape = [jax.ShapeDtypeStruct((t, D_MODEL), F32), jax.ShapeDtypeStruct((t * ROW_TILES, LANES), F32),
                 jax.ShapeDtypeStruct((t, LANES), I32), jax.ShapeDtypeStruct((t, LANES), F32),
                 jax.ShapeDtypeStruct((1, LANES), F32)]
    return out_specs, out_shape


def _post_attn_kernel(npt, n_x, *refs):
    x_refs = refs[:n_x]
    (op_ref, os_ref, mod_ref, w_ref, g_ref, wr_ref, br_ref,
     xo_ref, h_ref, ri_ref, rg_ref, cnt_ref, carry_ref) = refs[n_x:]
    i = pl.program_id(0)
    y = _dot(jnp.where(i < npt, op_ref[...], os_ref[...]), w_ref[...])
    _router_tail(i, _read_rows(i, npt, x_refs), y, mod_ref[0], g_ref[...], wr_ref[...], br_ref[...],
                 xo_ref, h_ref, ri_ref, rg_ref, cnt_ref, carry_ref)


def _post_attn_call(xs, o_p, o_s, mod3, layer, w_out_bf, g_ffn, wr, br, npt, tps):
    t = sum(x.shape[0] for x in xs)
    full = lambda a: pl.BlockSpec(a.shape, lambda i: (0,) * a.ndim)
    out_specs, out_shape = _tail_specs(t)
    return pl.pallas_call(
        functools.partial(_post_attn_kernel, npt, len(xs)),
        grid=(t // TM,),
        in_specs=_row_specs(xs, npt) + [
                  pl.BlockSpec((TM, D_MODEL), lambda i: (jnp.minimum(i, npt - 1), 0)),
                  pl.BlockSpec((TM, D_MODEL), lambda i: (jnp.maximum(i - npt, 0), 0)),
                  pl.BlockSpec((1, 1, N_MOD * D_MODEL), lambda i: (layer * 3 + _cond_row(i, npt, tps), 0, 0)),
                  full(w_out_bf), full(g_ffn), full(wr), full(br)],
        out_specs=out_specs, out_shape=out_shape,
        scratch_shapes=[pltpu.VMEM((1, LANES), F32)],
        compiler_params=_cparams(1),
        name="attn_out_router",
    )(*xs, o_p, o_s, mod3, w_out_bf, g_ffn, wr, br)


def _pool_kernel(npt, tps, x_ref, xp_ref, xn_ref, mod_ref, gm_ref, wp_ref, ps_ref, g_ref, wr_ref, br_ref,
                 xo_ref, h_ref, ri_ref, rg_ref, cnt_ref, carry_ref):
    i = pl.program_id(0)
    mod = mod_ref[0]
    shift = mod[:, 0:D_MODEL]
    scale = mod[:, D_MODEL:2 * D_MODEL]
    gm = gm_ref[...]
    x = x_ref[...]
    pre = lambda v: _rms(v, gm) * (1.0 + scale) + shift
    j = jnp.where(i < npt, 0, (i - npt) % tps)
    ntile = jnp.where(i < npt, 1, tps)
    has_prev = (j > 0).astype(F32)
    has_next = (j < ntile - 1).astype(F32)
    h = pre(x)
    hc = jnp.concatenate([pre(xp_ref[...]) * has_prev, h, pre(xn_ref[...]) * has_next], axis=0)
    rows = hc.shape[0]
    t_seq = (j * TM + lax.broadcasted_iota(I32, (TM, 1), 0)).astype(F32)
    seq_len = (ntile * TM).astype(F32)
    ys = []
    for g, w in enumerate(POOL_WINDOWS):
        half = w // 2
        s = hc[:, g * POOL_CH:(g + 1) * POOL_CH]
        step = 1
        while step < w:
            s = s + pltpu.roll(s, rows - step, 0)
            step *= 2
        s = pltpu.roll(s, half, 0) if half != HALO else s
        win = s[HALO:HALO + TM] if half != HALO else s[0:TM]
        cnt = jnp.minimum(t_seq + half, seq_len) - jnp.maximum(t_seq - half, 0.0)
        pooled = win / cnt - h[:, g * POOL_CH:(g + 1) * POOL_CH]
        ys.append(_dot(pooled.astype(BF16), wp_ref[g]))
    y = jnp.concatenate(ys, axis=1) * ps_ref[...]
    _router_tail(i, x, y, mod, g_ref[...], wr_ref[...], br_ref[...],
                 xo_ref, h_ref, ri_ref, rg_ref, cnt_ref, carry_ref)


def _pool_call(x, mod3, layer, g_mix, w_pool_bf, pool_scale, g_ffn, wr, br, npt, tps):
    t = x.shape[0]
    per = TM // HALO
    nh = t // HALO
    row = lambda w: pl.BlockSpec((TM, w), lambda i: (i, 0))
    full = lambda a: pl.BlockSpec(a.shape, lambda i: (0,) * a.ndim)
    out_specs, out_shape = _tail_specs(t)
    return pl.pallas_call(
        functools.partial(_pool_kernel, npt, tps),
        grid=(t // TM,),
        in_specs=[row(D_MODEL),
                  pl.BlockSpec((HALO, D_MODEL), lambda i: (jnp.maximum(i * per - 1, 0), 0)),
                  pl.BlockSpec((HALO, D_MODEL), lambda i: (jnp.minimum((i + 1) * per, nh - 1), 0)),
                  pl.BlockSpec((1, 1, N_MOD * D_MODEL), lambda i: (layer * 3 + _cond_row(i, npt, tps), 0, 0)),
                  full(g_mix), full(w_pool_bf), full(pool_scale), full(g_ffn), full(wr), full(br)],
        out_specs=out_specs, out_shape=out_shape,
        scratch_shapes=[pltpu.VMEM((1, LANES), F32)],
        compiler_params=_cparams(1),
        name="pool_router",
    )(x, x, x, mod3, g_mix, w_pool_bf, pool_scale, g_ffn, wr, br)


def _expert_rows(x, e, wgu_bf, wd_bf, bgu_ref, bd_ref):
    gu = _dot(x, wgu_bf[...]) + bgu_ref[pl.ds(e, 1), :]
    gate = jnp.minimum(gu[:, :D_EXPERT], SWIGLU_LIMIT)
    up = jnp.clip(gu[:, D_EXPERT:], -SWIGLU_LIMIT, SWIGLU_LIMIT)
    act = gate * jax.nn.sigmoid(SWIGLU_ALPHA * gate) * (up + 1.0)
    return _dot(act.astype(BF16), wd_bf[...]) + bd_ref[pl.ds(e, 1), :]


ROW_GROUP = 8


def _for_rows(n, fn):
    def group(g, c):
        for q in range(ROW_GROUP):
            fn(g * ROW_GROUP + q, q % 2)
        return c

    def single(r, c):
        fn(r, 0)
        return c
    full = n // ROW_GROUP
    lax.fori_loop(0, full, group, 0)
    lax.fori_loop(full * ROW_GROUP, n, single, 0)


def _gmm_kernel(layer, n_tok, pos_ref, te_ref, nv_ref, nxt_ref,
                h_hbm, wgu_hbm, wd_hbm, bgu_ref, bd_ref, yk_hbm,
                src_ref, cur_ref, xbuf, ybuf, wgu_st, wd_st, wgu_bf, wd_bf, sem_g, sem_s, sem_w):
    w = pl.program_id(0)
    n_steps = pl.num_programs(0)
    slot = w % 2
    nv = nv_ref[w]
    nv_next = jnp.where(w + 1 < n_steps, nv_ref[jnp.minimum(w + 1, n_steps - 1)], 0)

    def weights_copy(e):
        return (pltpu.make_async_copy(wgu_hbm.at[layer, e], wgu_st, sem_w.at[0]),
                pltpu.make_async_copy(wd_hbm.at[layer, e], wd_st, sem_w.at[1]))

    def tile_of(ref, start):
        return ref.at[pl.ds(pl.multiple_of(start, ROW_TILES), ROW_TILES), :]

    def rows_of(ref, n):
        return ref.at[pl.ds(0, pl.multiple_of(n * ROW_TILES, ROW_TILES)), :]

    def gather_start(tile, dst_slot, n):
        def one(r, prio):
            tok_row = src_ref[tile * TM + r] & (n_tok * ROW_TILES - 1)
            pltpu.make_async_copy(tile_of(h_hbm, tok_row), tile_of(xbuf.at[dst_slot], r * ROW_TILES),
                                  sem_g.at[dst_slot]).start(priority=prio)
        _for_rows(n, one)

    def gather_wait(dst_slot, n):
        pltpu.make_async_copy(rows_of(h_hbm, n), rows_of(xbuf.at[dst_slot], n), sem_g.at[dst_slot]).wait()

    def scatter_start(tile, src_slot, n):
        def one(r, prio):
            pltpu.make_async_copy(tile_of(ybuf.at[src_slot], r * ROW_TILES),
                                  tile_of(yk_hbm, src_ref[tile * TM + r]),
                                  sem_s.at[src_slot]).start(priority=prio)
        _for_rows(n, one)

    def scatter_wait(src_slot, n):
        pltpu.make_async_copy(rows_of(ybuf.at[src_slot], n), rows_of(yk_hbm, n), sem_s.at[src_slot]).wait()

    @pl.when(w == 0)
    def _():
        def invert(a, c):
            src_ref[pos_ref[a]] = a * ROW_TILES
            return c
        lax.fori_loop(0, n_tok * TOP_K, invert, 0, unroll=8)
        xbuf[...] = jnp.zeros_like(xbuf)
        cur_ref[0] = -1
        for cp in weights_copy(te_ref[0]):
            cp.start()
        gather_start(0, 0, nv)

    @pl.when(nv > 0)
    def _():
        e = te_ref[w]
        gather_wait(slot, nv)

        @pl.when(nv_next > 0)
        def _():
            gather_start(w + 1, 1 - slot, nv_next)

        @pl.when(w >= 2)
        def _():
            scatter_wait(slot, nv_ref[jnp.maximum(w - 2, 0)])

        @pl.when(cur_ref[0] != e)
        def _():
            for cp in weights_copy(e):
                cp.wait()
            wgu_bf[...] = wgu_st[...].astype(BF16)
            wd_bf[...] = wd_st[...].astype(BF16)
            cur_ref[0] = e

            @pl.when(nxt_ref[e] < N_EXPERTS)
            def _():
                for cp in weights_copy(nxt_ref[e]):
                    cp.start()

        def run(rows):
            x = _load_token_tiles(xbuf.at[slot], rows).astype(BF16)
            _store_token_tiles(ybuf.at[slot], _expert_rows(x, e, wgu_bf, wd_bf, bgu_ref, bd_ref))

        @pl.when(nv > TM // 2)
        def _():
            run(TM)

        @pl.when(nv <= TM // 2)
        def _():
            run(TM // 2)

        scatter_start(w, slot, nv)

        @pl.when(nv_next == 0)
        def _():
            @pl.when(w >= 1)
            def _():
                scatter_wait(1 - slot, nv_ref[jnp.maximum(w - 1, 0)])
            scatter_wait(slot, nv)


def _gmm_call(pos, te, nv, nxt, h, w_gu, w_dn, b_gu, b_dn, layer):
    assert ROW_TILES == SUBLANES
    t = h.shape[0] // ROW_TILES
    assert t & (t - 1) == 0
    n_rows = t * TOP_K
    n_steps = te.shape[0]
    return pl.pallas_call(
        functools.partial(_gmm_kernel, layer, t),
        grid_spec=pltpu.PrefetchScalarGridSpec(
            num_scalar_prefetch=4, grid=(n_steps,),
            in_specs=[pl.BlockSpec(memory_space=pl.ANY), pl.BlockSpec(memory_space=pl.ANY),
                      pl.BlockSpec(memory_space=pl.ANY),
                      pl.BlockSpec((None, N_EXPERTS, 2 * D_EXPERT), lambda w, *_: (layer, 0, 0)),
                      pl.BlockSpec((None, N_EXPERTS, D_MODEL), lambda w, *_: (layer, 0, 0))],
            out_specs=pl.BlockSpec(memory_space=pl.ANY),
            scratch_shapes=[
                pltpu.SMEM((n_steps * TM,), I32), pltpu.SMEM((1,), I32),
                pltpu.VMEM((2, TM * ROW_TILES, LANES), F32), pltpu.VMEM((2, TM * ROW_TILES, LANES), F32),
                pltpu.VMEM((D_MODEL, 2 * D_EXPERT), F32), pltpu.VMEM((D_EXPERT, D_MODEL), F32),
                pltpu.VMEM((D_MODEL, 2 * D_EXPERT), BF16), pltpu.VMEM((D_EXPERT, D_MODEL), BF16),
                pltpu.SemaphoreType.DMA((2,)), pltpu.SemaphoreType.DMA((2,)), pltpu.SemaphoreType.DMA((2,))]),
        out_shape=jax.ShapeDtypeStruct((n_rows * ROW_TILES, LANES), F32),
        compiler_params=_cparams(1),
        name="moe_experts",
    )(pos, te, nv, nxt, h, w_gu, w_dn, b_gu, b_dn)


def _combine_kernel(final, npt, x_ref, rg_ref, mod_ref, gf_ref, y0_ref, y1_ref, y2_ref, y3_ref, *outs):
    i = pl.program_id(0)
    rg = rg_ref[...]
    moe = rg[:, 0:1] * _load_token_tiles(y0_ref, TM)
    for k, y_ref in enumerate((y1_ref, y2_ref, y3_ref), start=1):
        moe = moe + rg[:, k:k + 1] * _load_token_tiles(y_ref, TM)
    gate_ffn = mod_ref[0][:, 5 * D_MODEL:6 * D_MODEL]
    xn = x_ref[...] + gate_ffn * moe
    if not final:
        outs[0][...] = xn
        return
    y = _rms(xn, gf_ref[...])
    yp_ref, ys_ref = outs

    @pl.when(i < npt)
    def _():
        yp_ref[...] = y

    @pl.when(i >= npt)
    def _():
        ys_ref[...] = y


def _combine_call(yk, x, rg, mod3, layer, g_final, final, npt, tps):
    t = x.shape[0]
    nt = t // TM
    assert TOP_K == 4
    row = lambda w: pl.BlockSpec((TM, w), lambda i: (i, 0))
    slab = lambda k: pl.BlockSpec((TM * ROW_TILES, LANES), lambda i: (k * nt + i, 0))
    if final:
        out_specs = [pl.BlockSpec((TM, D_MODEL), lambda i: (jnp.minimum(i, npt - 1), 0)),
                     pl.BlockSpec((TM, D_MODEL), lambda i: (jnp.maximum(i - npt, 0), 0))]
        out_shape = [jax.ShapeDtypeStruct((npt * TM, D_MODEL), F32),
                     jax.ShapeDtypeStruct(((nt - npt) * TM, D_MODEL), F32)]
    else:
        out_specs = [row(D_MODEL)]
        out_shape = [jax.ShapeDtypeStruct((t, D_MODEL), F32)]
    return pl.pallas_call(
        functools.partial(_combine_kernel, final, npt),
        grid=(nt,),
        in_specs=[row(D_MODEL), row(LANES),
                  pl.BlockSpec((1, 1, N_MOD * D_MODEL), lambda i: (layer * 3 + _cond_row(i, npt, tps), 0, 0)),
                  pl.BlockSpec(g_final.shape, lambda i: (0, 0))] + [slab(k) for k in range(TOP_K)],
        out_specs=out_specs, out_shape=out_shape,
        compiler_params=_cparams(1),
        name="moe_combine",
    )(x, rg, mod3, g_final, yk, yk, yk, yk)


def _routing_tables(ri, cnt, n_tok):
    ex = jnp.arange(N_EXPERTS, dtype=I32)
    counts = cnt[0, :N_EXPERTS].astype(I32)
    tiles = (counts + (TM - 1)) // TM
    tile_end = jnp.sum(jnp.where(ex[None, :] <= ex[:, None], tiles[None, :], 0), axis=1)
    tile_start = tile_end - tiles
    e_idx = ri[:, :TOP_K]
    rank = ri[:, TOP_K:2 * TOP_K]
    onehot = e_idx[:, :, None] == ex[None, None, :]
    pos = rank + jnp.sum(jnp.where(onehot, tile_start[None, None, :] * TM, 0), axis=-1)
    pos = pos.T.reshape(-1).astype(I32)
    n_steps = n_tok * TOP_K // TM + N_EXPERTS
    w = jnp.arange(n_steps, dtype=I32)
    te = jnp.minimum(jnp.sum((tile_end[None, :] <= w[:, None]).astype(I32), axis=1), N_EXPERTS - 1)
    mine = te[:, None] == ex[None, :]
    left = jnp.sum(jnp.where(mine, (counts - (w[:, None] - tile_start[None, :]) * TM)[...], 0), axis=1)
    nv = jnp.where(w < tile_end[-1], jnp.clip(left, 0, TM), 0)
    later = (ex[None, :] > ex[:, None]) & (counts[None, :] > 0)
    nxt = jnp.min(jnp.where(later, ex[None, :], N_EXPERTS), axis=1)
    return pos, te.astype(I32), nv.astype(I32), nxt.astype(I32)


def _moe(x, h, ri, rg, cnt, mod3, layer, w_gu, b_gu, w_dn, b_dn, g_final, final, npt, tps):
    pos, te, nv, nxt = _routing_tables(ri, cnt, x.shape[0])
    yk = _gmm_call(pos, te, nv, nxt, h, w_gu, w_dn, b_gu, b_dn, layer)
    return _combine_call(yk, x, rg, mod3, layer, g_final, final, npt, tps)


def _rope_tables(n_tokens):
    n_rows = n_tokens // GRID_W
    rows = jnp.repeat(jnp.arange(n_rows, dtype=F32), GRID_W)
    cols = jnp.tile(jnp.arange(GRID_W, dtype=F32), n_rows)
    axis_dim = A_HEAD_DIM // 2
    inv = ROPE_THETA ** (-jnp.arange(0, axis_dim, 2, dtype=F32) / axis_dim)
    ang_r = rows[:, None] * inv[None, :]
    ang_c = cols[:, None] * inv[None, :]
    ang = jnp.concatenate([ang_r, ang_r, ang_c, ang_c], axis=-1)
    cos, sin = jnp.cos(ang), jnp.sin(ang)
    sign = jnp.where((jnp.arange(A_HEAD_DIM) % 32) < 16, -1.0, 1.0).astype(F32)
    sin = sin * sign[None, :]
    cos = jnp.concatenate([jnp.ones((TM, A_HEAD_DIM), F32), cos], axis=0)
    sin = jnp.concatenate([jnp.zeros((TM, A_HEAD_DIM), F32), sin], axis=0)
    return jnp.tile(cos, (1, LANES // A_HEAD_DIM)), jnp.tile(sin, (1, LANES // A_HEAD_DIM))


def kernel(x_prompt, x_sample, cache_diff_k, cache_diff_v, cache_gqa_k, cache_gqa_v, c, c_ctx, w_ada, b_ada, norm_mix, norm_ffn, norm_final, w_attn_in, w_attn_out, lam_q1, lam_k1, lam_q2, lam_k2, diff_subln, gqa_q_norm, gqa_k_norm, w_pool, pool_scale, w_router, b_router, w_gate_up, b_gate_up, w_down, b_down):
    n_prompt, prompt_len, d = x_prompt.shape
    n_sample, sample_len, _ = x_sample.shape
    assert d == D_MODEL and prompt_len % TM == 0 and sample_len % TM == 0
    assert n_sample + 1 <= SUBLANES
    npt = n_prompt * prompt_len // TM
    tps = sample_len // TM
    tp = n_prompt * prompt_len
    past = cache_diff_k.shape[2]

    xs = (x_prompt.reshape(tp, d), x_sample.reshape(n_sample * sample_len, d))
    cond8 = jnp.zeros((SUBLANES, d), F32).at[0].set(c_ctx).at[1:1 + n_sample].set(c)
    mod = _modulation_all(cond8, w_ada, b_ada)
    mod3 = mod[:, :1 + n_sample].reshape(DEPTH * (1 + n_sample), 1, N_MOD * d)
    assert n_sample == 2

    cos_t, sin_t = _rope_tables(sample_len)
    seg_r = jnp.arange(BQ_W)[:, None] // B_HEAD_DIM
    seg = (seg_r == seg_r.T).astype(BF16)

    caches = []
    y_final = None
    for i in range(DEPTH):
        j = i // 2
        g_mix = norm_mix[i][None]
        g_ffn = norm_ffn[i][None]
        wr = jnp.zeros((d, LANES), F32).at[:, :N_EXPERTS].set(w_router[i])
        br = jnp.full((1, LANES), NEG_BIG, F32).at[0, :N_EXPERTS].set(b_router[i])
        if i % 2 == 0:
            lambda_init = 0.8 - 0.6 * math.exp(-0.3 * i)
            gq = jnp.tile(gqa_q_norm[j], BQ_W // B_HEAD_DIM)[None]
            gk = jnp.tile(gqa_k_norm[j], BKV_W // B_HEAD_DIM)[None]
            qa, ka, va, qb, kbd, vb, ck, cv, cgk, cgv = _qkv_call(
                xs, mod3, i, g_mix, w_attn_in[j].astype(BF16), gq, gk, cos_t, sin_t, seg, npt, tps)
            caches.append((ck, cv, cgk, cgv))
            cka = cache_diff_k[:, j].reshape(n_sample, past, A_W).astype(BF16)
            cva = cache_diff_v[:, j].reshape(n_sample, past, A_W).astype(BF16)
            gk_c = cache_gqa_k[:, j]
            ckbd = jnp.concatenate([gk_c[:, :, 0], gk_c[:, :, 0], gk_c[:, :, 1], gk_c[:, :, 1]],
                                   axis=-1).astype(BF16)
            cvb = cache_gqa_v[:, j].reshape(n_sample, past, BKV_W).astype(BF16)
            o_p, o_s = _attention(lambda_init, qa, ka, va, qb, kbd, vb, cka, cva, ckbd, cvb,
                                  lam_q1[j][None], lam_k1[j][None], lam_q2[j][None], lam_k2[j][None],
                                  diff_subln[j][None], n_prompt, prompt_len, n_sample, sample_len)
            x, h, ri, rg, cnt = _post_attn_call(xs, o_p, o_s, mod3, i, w_attn_out[j].astype(BF16), g_ffn, wr, br,
                                                npt, tps)
        else:
            x, h, ri, rg, cnt = _pool_call(x, mod3, i, g_mix, w_pool[j].astype(BF16), pool_scale[j][None],
                                           g_ffn, wr, br, npt, tps)
        final = i == DEPTH - 1
        outs = _moe(x, h, ri, rg, cnt, mod3, i, w_gate_up, b_gate_up, w_down, b_down, norm_final[None], final, npt, tps)
        if final:
            y_final = outs
        else:
            x = outs[0]
            xs = (x,)

    y_prompt = y_final[0].reshape(n_prompt, prompt_len, d)
    y_sample = y_final[1].reshape(n_sample, sample_len, d)
    stack = lambda k, shp: jnp.stack([cc[k].reshape(shp) for cc in caches], axis=1)
    new_diff_k = stack(0, (n_prompt, prompt_len, A_HEADS, 2 * A_HEAD_DIM))
    new_diff_v = stack(1, (n_prompt, prompt_len, A_HEADS, 2 * A_HEAD_DIM))
    new_gqa_k = stack(2, (n_prompt, prompt_len, B_KV_HEADS, B_HEAD_DIM))
    new_gqa_v = stack(3, (n_prompt, prompt_len, B_KV_HEADS, B_HEAD_DIM))
    return (y_prompt, y_sample, new_diff_k, new_diff_v, new_gqa_k, new_gqa_v)
```

```python
import functools
import math

import jax
import jax.numpy as jnp
from jax import lax
from jax.experimental import pallas as pl
from jax.experimental.pallas import tpu as pltpu

F32 = jnp.float32
BF16 = jnp.bfloat16
I32 = jnp.int32

D_MODEL = 1024
DEPTH = 4
GRID_W = 64
A_HEADS = 4
A_HEAD_DIM = 64
B_Q_HEADS = 8
B_KV_HEADS = 2
B_HEAD_DIM = 64
ROPE_THETA = 10000.0
A_W = A_HEADS * 2 * A_HEAD_DIM
BQ_W = B_Q_HEADS * B_HEAD_DIM
BKV_W = B_KV_HEADS * B_HEAD_DIM
IN_W = 3 * A_W + BQ_W + 2 * BKV_W
POOL_WINDOWS = (2, 4, 8, 16)
POOL_GROUPS = 4
POOL_CH = D_MODEL // POOL_GROUPS
N_EXPERTS = 32
TOP_K = 4
D_EXPERT = D_MODEL
SWIGLU_LIMIT = 7.0
SWIGLU_ALPHA = 1.702
N_MOD = 6
EPS = 1e-6

LANES = 128
SUBLANES = 8
TM = 256
ROW_TILES = D_MODEL // LANES
HALO = 8
ADA_TN = 1536
VMEM_LIMIT = 56 * 1024 * 1024
NEG_BIG = -1e30


def _cparams(n_axes, vmem=VMEM_LIMIT):
    return pltpu.CompilerParams(
        dimension_semantics=("arbitrary",) * n_axes, vmem_limit_bytes=vmem)


def _dot(a, b):
    return jnp.dot(a, b, preferred_element_type=F32)


def _dot_nt(a, b):
    return lax.dot_general(a, b, (((1,), (1,)), ((), ())), preferred_element_type=F32)


def _rms(x, g):
    ms = jnp.mean(x * x, axis=-1, keepdims=True)
    return x * lax.rsqrt(ms + EPS) * g


def _lane_iota(shape):
    return lax.broadcasted_iota(I32, shape, len(shape) - 1)


def _store_token_tiles(ref, x):
    n = x.shape[0]
    for c in range(ROW_TILES):
        ref[pl.ds(c, n, stride=ROW_TILES), :] = x[:, c * LANES:(c + 1) * LANES]


def _load_token_tiles(ref, n):
    return jnp.concatenate([ref[pl.ds(c, n, stride=ROW_TILES), :] for c in range(ROW_TILES)], axis=1)


def _ada_kernel(c_ref, w_ref, b_ref, o_ref):
    c = c_ref[...]
    s = (c * jax.nn.sigmoid(c)).astype(BF16)
    o_ref[0] = _dot(s, w_ref[0].astype(BF16)) + b_ref[0]


def _modulation_all(cond8, w_ada, b_ada):
    nmod = w_ada.shape[-1]
    return pl.pallas_call(
        _ada_kernel,
        grid=(DEPTH, nmod // ADA_TN),
        in_specs=[
            pl.BlockSpec((SUBLANES, D_MODEL), lambda l, n: (0, 0)),
            pl.BlockSpec((1, D_MODEL, ADA_TN), lambda l, n: (l, 0, n)),
            pl.BlockSpec((1, 1, ADA_TN), lambda l, n: (l, 0, n)),
        ],
        out_specs=pl.BlockSpec((1, SUBLANES, ADA_TN), lambda l, n: (l, 0, n)),
        out_shape=jax.ShapeDtypeStruct((DEPTH, SUBLANES, nmod), F32),
        compiler_params=_cparams(2),
        name="modulation",
    )(cond8, w_ada, b_ada.reshape(DEPTH, 1, nmod))


def _seg_meansq(x, seg):
    x2 = x * x
    hi = x2.astype(BF16)
    lo = (x2 - hi.astype(F32)).astype(BF16)
    return (_dot(hi, seg) + _dot(lo, seg)) * (1.0 / B_HEAD_DIM)


def _rope128(x, cos, sin_signed):
    lo16 = (_lane_iota(x.shape) % 32) < 16
    nxt = pltpu.roll(x, LANES - 16, 1)
    prv = pltpu.roll(x, 16, 1)
    return x * cos + jnp.where(lo16, nxt, prv) * sin_signed


def _rope(x, cos, sin_signed):
    cols = [_rope128(x[:, c:c + LANES], cos, sin_signed) for c in range(0, x.shape[1], LANES)]
    return cols[0] if len(cols) == 1 else jnp.concatenate(cols, axis=1)


def _row_specs(xs, npt):
    if len(xs) == 1:
        return [pl.BlockSpec((TM, D_MODEL), lambda i: (i, 0))]
    return [pl.BlockSpec((TM, D_MODEL), lambda i: (jnp.minimum(i, npt - 1), 0)),
            pl.BlockSpec((TM, D_MODEL), lambda i: (jnp.maximum(i - npt, 0), 0))]


def _read_rows(i, npt, x_refs):
    if len(x_refs) == 1:
        return x_refs[0][...]
    return jnp.where(i < npt, x_refs[0][...], x_refs[1][...])


def _qkv_kernel(n_prompt_tiles, n_x, *refs):
    x_refs = refs[:n_x]
    (mod_ref, g_ref, w_ref, gq_ref, gk_ref, cos_ref, sin_ref, seg_ref,
     qa_ref, ka_ref, va_ref, qb_ref, kbd_ref, vb_ref, ck_ref, cv_ref, cgk_ref, cgv_ref) = refs[n_x:]
    i = pl.program_id(0)
    mod = mod_ref[0]
    shift = mod[:, 0:D_MODEL]
    scale = mod[:, D_MODEL:2 * D_MODEL]
    h = _rms(_read_rows(i, n_prompt_tiles, x_refs), g_ref[...]) * (1.0 + scale) + shift
    p = _dot(h.astype(BF16), w_ref[...])
    a_q = p[:, 0:A_W]
    a_k = p[:, A_W:2 * A_W]
    a_v = p[:, 2 * A_W:3 * A_W]
    o = 3 * A_W
    b_q = p[:, o:o + BQ_W]
    b_k = p[:, o + BQ_W:o + BQ_W + BKV_W]
    b_v = p[:, o + BQ_W + BKV_W:o + BQ_W + 2 * BKV_W]
    seg = seg_ref[...]
    b_q = b_q * lax.rsqrt(_seg_meansq(b_q, seg) + EPS) * gq_ref[...]
    b_k = b_k * lax.rsqrt(_seg_meansq(b_k, seg[:BKV_W, :BKV_W]) + EPS) * gk_ref[...]

    @pl.when(i < n_prompt_tiles)
    def _():
        ck_ref[...] = a_k
        cv_ref[...] = a_v
        cgk_ref[...] = b_k
        cgv_ref[...] = b_v

    cos = cos_ref[...]
    sin = sin_ref[...]
    sm = A_HEAD_DIM ** -0.5
    qa_ref[...] = (_rope(a_q, cos, sin) * sm).astype(BF16)
    ka_ref[...] = _rope(a_k, cos, sin).astype(BF16)
    va_ref[...] = a_v.astype(BF16)
    qb_ref[...] = (_rope(b_q, cos, sin) * (B_HEAD_DIM ** -0.5)).astype(BF16)
    kb = _rope(b_k, cos, sin)
    kb_sw = pltpu.roll(kb, B_HEAD_DIM, 1)
    lo = _lane_iota(kb.shape) < B_HEAD_DIM
    kbd_ref[...] = jnp.concatenate(
        [jnp.where(lo, kb, kb_sw), jnp.where(lo, kb_sw, kb)], axis=1).astype(BF16)
    vb_ref[...] = b_v.astype(BF16)


def _cond_row(i, npt, tiles_per_seq):
    return jnp.where(i < npt, 0, 1 + (i - npt) // tiles_per_seq)


def _qkv_call(xs, mod3, layer, g_mix, w_in_bf, gq, gk, cos_t, sin_t, seg, npt, tps):
    t = sum(x.shape[0] for x in xs)
    nt = t // TM
    tp = npt * TM
    row = lambda w: pl.BlockSpec((TM, w), lambda i: (i, 0))
    full = lambda a: pl.BlockSpec(a.shape, lambda i: (0,) * a.ndim)
    tab = pl.BlockSpec((TM, LANES), lambda i: (jnp.where(i < npt, 0, 1 + (i - npt) % tps), 0))
    cache = lambda w: pl.BlockSpec((TM, w), lambda i: (jnp.minimum(i, npt - 1), 0))
    return pl.pallas_call(
        functools.partial(_qkv_kernel, npt, len(xs)),
        grid=(nt,),
        in_specs=_row_specs(xs, npt) + [
            pl.BlockSpec((1, 1, N_MOD * D_MODEL), lambda i: (layer * 3 + _cond_row(i, npt, tps), 0, 0)),
            full(g_mix), full(w_in_bf), full(gq), full(gk), tab, tab, full(seg),
        ],
        out_specs=[row(A_W), row(A_W), row(A_W), row(BQ_W), row(2 * BKV_W), row(BKV_W),
                   cache(A_W), cache(A_W), cache(BKV_W), cache(BKV_W)],
        out_shape=[
            jax.ShapeDtypeStruct((t, A_W), BF16), jax.ShapeDtypeStruct((t, A_W), BF16),
            jax.ShapeDtypeStruct((t, A_W), BF16), jax.ShapeDtypeStruct((t, BQ_W), BF16),
            jax.ShapeDtypeStruct((t, 2 * BKV_W), BF16), jax.ShapeDtypeStruct((t, BKV_W), BF16),
            jax.ShapeDtypeStruct((tp, A_W), F32), jax.ShapeDtypeStruct((tp, A_W), F32),
            jax.ShapeDtypeStruct((tp, BKV_W), F32), jax.ShapeDtypeStruct((tp, BKV_W), F32),
        ],
        compiler_params=_cparams(1),
        name="attn_qkv",
    )(*xs, mod3, g_mix, w_in_bf, gq, gk, cos_t, sin_t, seg)


def _softmax_pv(qq, ks, vs):
    ss = [_dot_nt(qq, k) for k in ks]
    m = ss[0].max(axis=-1, keepdims=True)
    for s in ss[1:]:
        m = jnp.maximum(m, s.max(axis=-1, keepdims=True))
    acc = None
    l = None
    for s, v in zip(ss, vs):
        p = jnp.exp(s - m)
        ls = p.sum(axis=-1, keepdims=True)
        pv = _dot(p.astype(BF16), v)
        acc = pv if acc is None else acc + pv
        l = ls if l is None else l + ls
    return acc / l


def _split_halves(q):
    lo = _lane_iota(q.shape) < (LANES // 2)
    zero = jnp.zeros_like(q)
    return jnp.concatenate([jnp.where(lo, q, zero), jnp.where(lo, zero, q)], axis=0)


def _attn_body(lambda_init, qa, qb, ka_segs, va_segs, kbd_segs, vb_segs, lam, subln):
    tq = qa.shape[0]
    cols = []
    for h in range(A_HEADS):
        sl = slice(h * LANES, (h + 1) * LANES)
        o = _softmax_pv(_split_halves(qa[:, sl]), [k[:, sl] for k in ka_segs],
                        [v[:, sl] for v in va_segs])
        od = o[:tq] - lam * o[tq:]
        cols.append(_rms(od, subln) * (1.0 - lambda_init))
    lo = _lane_iota((tq, LANES)) < B_HEAD_DIM
    for c in range(BQ_W // LANES):
        g = (2 * c) // (B_Q_HEADS // B_KV_HEADS)
        sl = slice(c * LANES, (c + 1) * LANES)
        gl = slice(g * LANES, (g + 1) * LANES)
        o = _softmax_pv(_split_halves(qb[:, sl]), [k[:, gl] for k in kbd_segs], vb_segs)
        oe, oo = o[:tq], o[tq:]
        if g == 0:
            cols.append(jnp.where(lo, oe, pltpu.roll(oo, B_HEAD_DIM, 1)))
        else:
            cols.append(jnp.where(lo, pltpu.roll(oe, B_HEAD_DIM, 1), oo))
    return jnp.concatenate(cols, axis=1).astype(BF16)


def _lambda(lq1, lk1, lq2, lk2, lambda_init):
    return (jnp.exp(jnp.sum(lq1 * lk1, axis=-1, keepdims=True))
            - jnp.exp(jnp.sum(lq2 * lk2, axis=-1, keepdims=True)) + lambda_init)


def _attn_prompt_kernel(lambda_init, qa, ka, va, qb, kbd, vb, lq1, lk1, lq2, lk2, subln, o_ref):
    lam = _lambda(lq1[...], lk1[...], lq2[...], lk2[...], lambda_init)
    o_ref[...] = _attn_body(lambda_init, qa[...], qb[...], [ka[...]], [va[...]],
                            [kbd[...]], [vb[...]], lam, subln[...])


def _attn_sample_kernel(lambda_init, qa, ka, va, qb, kbd, vb, cka, cva, ckbd, cvb,
                        lq1, lk1, lq2, lk2, subln, o_ref):
    lam = _lambda(lq1[...], lk1[...], lq2[...], lk2[...], lambda_init)
    o_ref[...] = _attn_body(lambda_init, qa[...], qb[...], [ka[...], cka[...]], [va[...], cva[...]],
                            [kbd[...], ckbd[...]], [vb[...], cvb[...]], lam, subln[...])


def _attention(lambda_init, qa, ka, va, qb, kbd, vb, cka, cva, ckbd, cvb,
               lq1, lk1, lq2, lk2, subln, n_prompt, prompt_len, n_sample, sample_len):
    npt = n_prompt * prompt_len // TM
    small = [lq1, lk1, lq2, lk2, subln]
    widths = [A_W, A_W, A_W, BQ_W, 2 * BKV_W, BKV_W]
    assert prompt_len == TM
    full1 = lambda a: pl.BlockSpec(a.shape, lambda b: (0,) * a.ndim)
    o_p = pl.pallas_call(
        functools.partial(_attn_prompt_kernel, lambda_init),
        grid=(n_prompt,),
        in_specs=[pl.BlockSpec((TM, w), lambda b: (b, 0)) for w in widths] + [full1(a) for a in small],
        out_specs=pl.BlockSpec((TM, D_MODEL), lambda b: (b, 0)),
        out_shape=jax.ShapeDtypeStruct((npt * TM, D_MODEL), BF16),
        compiler_params=_cparams(1),
        name="attn_prompt",
    )(qa, ka, va, qb, kbd, vb, *small)
    tps = sample_len // TM
    seq0 = n_prompt * prompt_len // sample_len
    assert seq0 * sample_len == n_prompt * prompt_len
    past = cka.shape[1]
    qspec = lambda w: pl.BlockSpec((TM, w), lambda b, q: (npt + b * tps + q, 0))
    kspec = lambda w: pl.BlockSpec((sample_len, w), lambda b, q: (seq0 + b, 0))
    cspec = lambda w: pl.BlockSpec((None, past, w), lambda b, q: (b, 0, 0))
    full2 = lambda a: pl.BlockSpec(a.shape, lambda b, q: (0,) * a.ndim)
    o_s = pl.pallas_call(
        functools.partial(_attn_sample_kernel, lambda_init),
        grid=(n_sample, tps),
        in_specs=[qspec(A_W), kspec(A_W), kspec(A_W), qspec(BQ_W), kspec(2 * BKV_W), kspec(BKV_W),
                  cspec(A_W), cspec(A_W), cspec(2 * BKV_W), cspec(BKV_W)] + [full2(a) for a in small],
        out_specs=pl.BlockSpec((TM, D_MODEL), lambda b, q: (b * tps + q, 0)),
        out_shape=jax.ShapeDtypeStruct((n_sample * sample_len, D_MODEL), BF16),
        compiler_params=_cparams(2),
        name="attn_sample",
    )(qa, ka, va, qb, kbd, vb, cka, cva, ckbd, cvb, *small)
    return o_p, o_s


def _router_tail(i, x, y, mod, g_ffn, wr, br, xo_ref, h_ref, ri_ref, rg_ref, cnt_ref, carry_ref):
    gate_mix = mod[:, 2 * D_MODEL:3 * D_MODEL]
    shift = mod[:, 3 * D_MODEL:4 * D_MODEL]
    scale = mod[:, 4 * D_MODEL:5 * D_MODEL]
    xn = x + gate_mix * y
    xo_ref[...] = xn
    h = _rms(xn, g_ffn) * (1.0 + scale) + shift
    _store_token_tiles(h_ref, h)
    h_hi = h.astype(BF16)
    h_lo = (h - h_hi.astype(F32)).astype(BF16)
    w_hi = wr.astype(BF16)
    w_lo = (wr - w_hi.astype(F32)).astype(BF16)
    logits = _dot(h_hi, w_hi) + _dot(h_lo, w_hi) + _dot(h_hi, w_lo) + br
    lane = _lane_iota(logits.shape)
    lane_f = lane.astype(F32)
    vals, idxs = [], []
    l = logits
    for _ in range(TOP_K):
        m = l.max(axis=-1, keepdims=True)
        idx = jnp.where(l == m, lane_f, float(LANES)).min(axis=-1, keepdims=True)
        vals.append(m)
        idxs.append(idx)
        l = jnp.where(lane_f == idx, -jnp.inf, l)
    es = [jnp.exp(v - vals[0]) for v in vals]
    den = es[0]
    for e in es[1:]:
        den = den + e
    sel = jnp.zeros(logits.shape, F32)
    for idx in idxs:
        sel = sel + jnp.where(lane_f == idx, 1.0, 0.0)

    @pl.when(i == 0)
    def _():
        carry_ref[...] = jnp.zeros_like(carry_ref)

    carry = carry_ref[...]
    r_io = lax.broadcasted_iota(I32, (TM, TM), 0)
    c_io = lax.broadcasted_iota(I32, (TM, TM), 1)
    tri = jnp.where(c_io < r_io, 1.0, 0.0).astype(BF16)
    rank = _dot(tri, sel.astype(BF16)) + carry
    carry = carry + sel.sum(axis=0, keepdims=True)
    carry_ref[...] = carry
    cnt_ref[...] = carry
    ri = jnp.zeros(logits.shape, F32)
    rg = jnp.zeros(logits.shape, F32)
    for k in range(TOP_K):
        rk = jnp.where(lane_f == idxs[k], rank, 0.0).sum(axis=-1, keepdims=True)
        ri = jnp.where(lane == k, idxs[k], ri)
        ri = jnp.where(lane == TOP_K + k, rk, ri)
        rg = jnp.where(lane == k, es[k] / den, rg)
    ri_ref[...] = ri.astype(I32)
    rg_ref[...] = rg


def _tail_specs(t):
    row = lambda w: pl.BlockSpec((TM, w), lambda i: (i, 0))
    out_specs = [row(D_MODEL), pl.BlockSpec((TM * ROW_TILES, LANES), lambda i: (i, 0)), row(LANES), row(LANES),
                 pl.BlockSpec((1, LANES), lambda i: (0, 0))]
    out_shape = [jax.ShapeDtypeStruct((t, D_MODEL), F32), jax.ShapeDtypeStruct((t * ROW_TILES, LANES), F32),
                 jax.ShapeDtypeStruct((t, LANES), I32), jax.ShapeDtypeStruct((t, LANES), F32),
                 jax.ShapeDtypeStruct((1, LANES), F32)]
    return out_specs, out_shape


def _post_attn_kernel(npt, n_x, *refs):
    x_refs = refs[:n_x]
    (op_ref, os_ref, mod_ref, w_ref, g_ref, wr_ref, br_ref,
     xo_ref, h_ref, ri_ref, rg_ref, cnt_ref, carry_ref) = refs[n_x:]
    i = pl.program_id(0)
    y = _dot(jnp.where(i < npt, op_ref[...], os_ref[...]), w_ref[...])
    _router_tail(i, _read_rows(i, npt, x_refs), y, mod_ref[0], g_ref[...], wr_ref[...], br_ref[...],
                 xo_ref, h_ref, ri_ref, rg_ref, cnt_ref, carry_ref)


def _post_attn_call(xs, o_p, o_s, mod3, layer, w_out_bf, g_ffn, wr, br, npt, tps):
    t = sum(x.shape[0] for x in xs)
    full = lambda a: pl.BlockSpec(a.shape, lambda i: (0,) * a.ndim)
    out_specs, out_shape = _tail_specs(t)
    return pl.pallas_call(
        functools.partial(_post_attn_kernel, npt, len(xs)),
        grid=(t // TM,),
        in_specs=_row_specs(xs, npt) + [
                  pl.BlockSpec((TM, D_MODEL), lambda i: (jnp.minimum(i, npt - 1), 0)),
                  pl.BlockSpec((TM, D_MODEL), lambda i: (jnp.maximum(i - npt, 0), 0)),
                  pl.BlockSpec((1, 1, N_MOD * D_MODEL), lambda i: (layer * 3 + _cond_row(i, npt, tps), 0, 0)),
                  full(w_out_bf), full(g_ffn), full(wr), full(br)],
        out_specs=out_specs, out_shape=out_shape,
        scratch_shapes=[pltpu.VMEM((1, LANES), F32)],
        compiler_params=_cparams(1),
        name="attn_out_router",
    )(*xs, o_p, o_s, mod3, w_out_bf, g_ffn, wr, br)


def _pool_kernel(npt, tps, x_ref, xp_ref, xn_ref, mod_ref, gm_ref, wp_ref, ps_ref, g_ref, wr_ref, br_ref,
                 xo_ref, h_ref, ri_ref, rg_ref, cnt_ref, carry_ref):
    i = pl.program_id(0)
    mod = mod_ref[0]
    shift = mod[:, 0:D_MODEL]
    scale = mod[:, D_MODEL:2 * D_MODEL]
    gm = gm_ref[...]
    x = x_ref[...]
    pre = lambda v: _rms(v, gm) * (1.0 + scale) + shift
    j = jnp.where(i < npt, 0, (i - npt) % tps)
    ntile = jnp.where(i < npt, 1, tps)
    has_prev = (j > 0).astype(F32)
    has_next = (j < ntile - 1).astype(F32)
    h = pre(x)
    hc = jnp.concatenate([pre(xp_ref[...]) * has_prev, h, pre(xn_ref[...]) * has_next], axis=0)
    rows = hc.shape[0]
    t_seq = (j * TM + lax.broadcasted_iota(I32, (TM, 1), 0)).astype(F32)
    seq_len = (ntile * TM).astype(F32)
    ys = []
    for g, w in enumerate(POOL_WINDOWS):
        half = w // 2
        s = hc[:, g * POOL_CH:(g + 1) * POOL_CH]
        step = 1
        while step < w:
            s = s + pltpu.roll(s, rows - step, 0)
            step *= 2
        s = pltpu.roll(s, half, 0) if half != HALO else s
        win = s[HALO:HALO + TM] if half != HALO else s[0:TM]
        cnt = jnp.minimum(t_seq + half, seq_len) - jnp.maximum(t_seq - half, 0.0)
        pooled = win / cnt - h[:, g * POOL_CH:(g + 1) * POOL_CH]
        ys.append(_dot(pooled.astype(BF16), wp_ref[g]))
    y = jnp.concatenate(ys, axis=1) * ps_ref[...]
    _router_tail(i, x, y, mod, g_ref[...], wr_ref[...], br_ref[...],
                 xo_ref, h_ref, ri_ref, rg_ref, cnt_ref, carry_ref)


def _pool_call(x, mod3, layer, g_mix, w_pool_bf, pool_scale, g_ffn, wr, br, npt, tps):
    t = x.shape[0]
    per = TM // HALO
    nh = t // HALO
    row = lambda w: pl.BlockSpec((TM, w), lambda i: (i, 0))
    full = lambda a: pl.BlockSpec(a.shape, lambda i: (0,) * a.ndim)
    out_specs, out_shape = _tail_specs(t)
    return pl.pallas_call(
        functools.partial(_pool_kernel, npt, tps),
        grid=(t // TM,),
        in_specs=[row(D_MODEL),
                  pl.BlockSpec((HALO, D_MODEL), lambda i: (jnp.maximum(i * per - 1, 0), 0)),
                  pl.BlockSpec((HALO, D_MODEL), lambda i: (jnp.minimum((i + 1) * per, nh - 1), 0)),
                  pl.BlockSpec((1, 1, N_MOD * D_MODEL), lambda i: (layer * 3 + _cond_row(i, npt, tps), 0, 0)),
                  full(g_mix), full(w_pool_bf), full(pool_scale), full(g_ffn), full(wr), full(br)],
        out_specs=out_specs, out_shape=out_shape,
        scratch_shapes=[pltpu.VMEM((1, LANES), F32)],
        compiler_params=_cparams(1),
        name="pool_router",
    )(x, x, x, mod3, g_mix, w_pool_bf, pool_scale, g_ffn, wr, br)


def _expert_rows(x, e, wgu_bf, wd_bf, bgu_ref, bd_ref):
    gu = _dot(x, wgu_bf[...]) + bgu_ref[pl.ds(e, 1), :]
    gate = jnp.minimum(gu[:, :D_EXPERT], SWIGLU_LIMIT)
    up = jnp.clip(gu[:, D_EXPERT:], -SWIGLU_LIMIT, SWIGLU_LIMIT)
    act = gate * jax.nn.sigmoid(SWIGLU_ALPHA * gate) * (up + 1.0)
    return _dot(act.astype(BF16), wd_bf[...]) + bd_ref[pl.ds(e, 1), :]


ROW_GROUP = 8


def _for_rows(n, fn):
    def group(g, c):
        for q in range(ROW_GROUP):
            fn(g * ROW_GROUP + q, q % 2)
        return c

    def single(r, c):
        fn(r, 0)
        return c
    full = n // ROW_GROUP
    lax.fori_loop(0, full, group, 0)
    lax.fori_loop(full * ROW_GROUP, n, single, 0)


def _gmm_kernel(layer, n_tok, pos_ref, te_ref, nv_ref, nxt_ref,
                h_hbm, wgu_hbm, wd_hbm, bgu_ref, bd_ref, yk_hbm,
                src_ref, cur_ref, xbuf, ybuf, wgu_st, wd_st, wgu_bf, wd_bf, sem_g, sem_s, sem_w):
    w = pl.program_id(0)
    n_steps = pl.num_programs(0)
    slot = w % 2
    nv = nv_ref[w]
    nv_next = jnp.where(w + 1 < n_steps, nv_ref[jnp.minimum(w + 1, n_steps - 1)], 0)

    def weights_copy(e):
        return (pltpu.make_async_copy(wgu_hbm.at[layer, e], wgu_st, sem_w.at[0]),
                pltpu.make_async_copy(wd_hbm.at[layer, e], wd_st, sem_w.at[1]))

    def tile_of(ref, start):
        return ref.at[pl.ds(pl.multiple_of(start, ROW_TILES), ROW_TILES), :]

    def rows_of(ref, n):
        return ref.at[pl.ds(0, pl.multiple_of(n * ROW_TILES, ROW_TILES)), :]

    def gather_start(tile, dst_slot, n):
        def one(r, prio):
            tok_row = src_ref[tile * TM + r] & (n_tok * ROW_TILES - 1)
            pltpu.make_async_copy(tile_of(h_hbm, tok_row), tile_of(xbuf.at[dst_slot], r * ROW_TILES),
                                  sem_g.at[dst_slot]).start(priority=prio)
        _for_rows(n, one)

    def gather_wait(dst_slot, n):
        pltpu.make_async_copy(rows_of(h_hbm, n), rows_of(xbuf.at[dst_slot], n), sem_g.at[dst_slot]).wait()

    def scatter_start(tile, src_slot, n):
        def one(r, prio):
            pltpu.make_async_copy(tile_of(ybuf.at[src_slot], r * ROW_TILES),
                                  tile_of(yk_hbm, src_ref[tile * TM + r]),
                                  sem_s.at[src_slot]).start(priority=prio)
        _for_rows(n, one)

    def scatter_wait(src_slot, n):
        pltpu.make_async_copy(rows_of(ybuf.at[src_slot], n), rows_of(yk_hbm, n), sem_s.at[src_slot]).wait()

    @pl.when(w == 0)
    def _():
        def invert(a, c):
            src_ref[pos_ref[a]] = a * ROW_TILES
            return c
        lax.fori_loop(0, n_tok * TOP_K, invert, 0, unroll=8)
        xbuf[...] = jnp.zeros_like(xbuf)
        cur_ref[0] = -1
        for cp in weights_copy(te_ref[0]):
            cp.start()
        gather_start(0, 0, nv)

    @pl.when(nv > 0)
    def _():
        e = te_ref[w]
        gather_wait(slot, nv)

        @pl.when(nv_next > 0)
        def _():
            gather_start(w + 1, 1 - slot, nv_next)

        @pl.when(w >= 2)
        def _():
            scatter_wait(slot, nv_ref[jnp.maximum(w - 2, 0)])

        @pl.when(cur_ref[0] != e)
        def _():
            for cp in weights_copy(e):
                cp.wait()
            wgu_bf[...] = wgu_st[...].astype(BF16)
            wd_bf[...] = wd_st[...].astype(BF16)
            cur_ref[0] = e

            @pl.when(nxt_ref[e] < N_EXPERTS)
            def _():
                for cp in weights_copy(nxt_ref[e]):
                    cp.start()

        def run(rows):
            x = _load_token_tiles(xbuf.at[slot], rows).astype(BF16)
            _store_token_tiles(ybuf.at[slot], _expert_rows(x, e, wgu_bf, wd_bf, bgu_ref, bd_ref))

        @pl.when(nv > TM // 2)
        def _():
            run(TM)

        @pl.when(nv <= TM // 2)
        def _():
            run(TM // 2)

        scatter_start(w, slot, nv)

        @pl.when(nv_next == 0)
        def _():
            @pl.when(w >= 1)
            def _():
                scatter_wait(1 - slot, nv_ref[jnp.maximum(w - 1, 0)])
            scatter_wait(slot, nv)


def _gmm_call(pos, te, nv, nxt, h, w_gu, w_dn, b_gu, b_dn, layer):
    assert ROW_TILES == SUBLANES
    t = h.shape[0] // ROW_TILES
    assert t & (t - 1) == 0
    n_rows = t * TOP_K
    n_steps = te.shape[0]
    return pl.pallas_call(
        functools.partial(_gmm_kernel, layer, t),
        grid_spec=pltpu.PrefetchScalarGridSpec(
            num_scalar_prefetch=4, grid=(n_steps,),
            in_specs=[pl.BlockSpec(memory_space=pl.ANY), pl.BlockSpec(memory_space=pl.ANY),
                      pl.BlockSpec(memory_space=pl.ANY),
                      pl.BlockSpec((None, N_EXPERTS, 2 * D_EXPERT), lambda w, *_: (layer, 0, 0)),
                      pl.BlockSpec((None, N_EXPERTS, D_MODEL), lambda w, *_: (layer, 0, 0))],
            out_specs=pl.BlockSpec(memory_space=pl.ANY),
            scratch_shapes=[
                pltpu.SMEM((n_steps * TM,), I32), pltpu.SMEM((1,), I32),
                pltpu.VMEM((2, TM * ROW_TILES, LANES), F32), pltpu.VMEM((2, TM * ROW_TILES, LANES), F32),
                pltpu.VMEM((D_MODEL, 2 * D_EXPERT), F32), pltpu.VMEM((D_EXPERT, D_MODEL), F32),
                pltpu.VMEM((D_MODEL, 2 * D_EXPERT), BF16), pltpu.VMEM((D_EXPERT, D_MODEL), BF16),
                pltpu.SemaphoreType.DMA((2,)), pltpu.SemaphoreType.DMA((2,)), pltpu.SemaphoreType.DMA((2,))]),
        out_shape=jax.ShapeDtypeStruct((n_rows * ROW_TILES, LANES), F32),
        compiler_params=_cparams(1),
        name="moe_experts",
    )(pos, te, nv, nxt, h, w_gu, w_dn, b_gu, b_dn)


def _combine_kernel(final, npt, x_ref, rg_ref, mod_ref, gf_ref, y0_ref, y1_ref, y2_ref, y3_ref, *outs):
    i = pl.program_id(0)
    rg = rg_ref[...]
    moe = rg[:, 0:1] * _load_token_tiles(y0_ref, TM)
    for k, y_ref in enumerate((y1_ref, y2_ref, y3_ref), start=1):
        moe = moe + rg[:, k:k + 1] * _load_token_tiles(y_ref, TM)
    gate_ffn = mod_ref[0][:, 5 * D_MODEL:6 * D_MODEL]
    xn = x_ref[...] + gate_ffn * moe
    if not final:
        outs[0][...] = xn
        return
    y = _rms(xn, gf_ref[...])
    yp_ref, ys_ref = outs

    @pl.when(i < npt)
    def _():
        yp_ref[...] = y

    @pl.when(i >= npt)
    def _():
        ys_ref[...] = y


def _combine_call(yk, x, rg, mod3, layer, g_final, final, npt, tps):
    t = x.shape[0]
    nt = t // TM
    assert TOP_K == 4
    row = lambda w: pl.BlockSpec((TM, w), lambda i: (i, 0))
    slab = lambda k: pl.BlockSpec((TM * ROW_TILES, LANES), lambda i: (k * nt + i, 0))
    if final:
        out_specs = [pl.BlockSpec((TM, D_MODEL), lambda i: (jnp.minimum(i, npt - 1), 0)),
                     pl.BlockSpec((TM, D_MODEL), lambda i: (jnp.maximum(i - npt, 0), 0))]
        out_shape = [jax.ShapeDtypeStruct((npt * TM, D_MODEL), F32),
                     jax.ShapeDtypeStruct(((nt - npt) * TM, D_MODEL), F32)]
    else:
        out_specs = [row(D_MODEL)]
        out_shape = [jax.ShapeDtypeStruct((t, D_MODEL), F32)]
    return pl.pallas_call(
        functools.partial(_combine_kernel, final, npt),
        grid=(nt,),
        in_specs=[row(D_MODEL), row(LANES),
                  pl.BlockSpec((1, 1, N_MOD * D_MODEL), lambda i: (layer * 3 + _cond_row(i, npt, tps), 0, 0)),
                  pl.BlockSpec(g_final.shape, lambda i: (0, 0))] + [slab(k) for k in range(TOP_K)],
        out_specs=out_specs, out_shape=out_shape,
        compiler_params=_cparams(1),
        name="moe_combine",
    )(x, rg, mod3, g_final, yk, yk, yk, yk)


def _routing_tables(ri, cnt, n_tok):
    ex = jnp.arange(N_EXPERTS, dtype=I32)
    counts = cnt[0, :N_EXPERTS].astype(I32)
    tiles = (counts + (TM - 1)) // TM
    tile_end = jnp.sum(jnp.where(ex[None, :] <= ex[:, None], tiles[None, :], 0), axis=1)
    tile_start = tile_end - tiles
    e_idx = ri[:, :TOP_K]
    rank = ri[:, TOP_K:2 * TOP_K]
    onehot = e_idx[:, :, None] == ex[None, None, :]
    pos = rank + jnp.sum(jnp.where(onehot, tile_start[None, None, :] * TM, 0), axis=-1)
    pos = pos.T.reshape(-1).astype(I32)
    n_steps = n_tok * TOP_K // TM + N_EXPERTS
    w = jnp.arange(n_steps, dtype=I32)
    te = jnp.minimum(jnp.sum((tile_end[None, :] <= w[:, None]).astype(I32), axis=1), N_EXPERTS - 1)
    mine = te[:, None] == ex[None, :]
    left = jnp.sum(jnp.where(mine, counts[None, :] - (w[:, None] - tile_start[None, :]) * TM, 0), axis=1)
    nv = jnp.where(w < tile_end[-1], jnp.clip(left, 0, TM), 0)
    later = (ex[None, :] > ex[:, None]) & (counts[None, :] > 0)
    nxt = jnp.min(jnp.where(later, ex[None, :], N_EXPERTS), axis=1)
    return pos, te.astype(I32), nv.astype(I32), nxt.astype(I32)


def _moe(x, h, ri, rg, cnt, mod3, layer, w_gu, b_gu, w_dn, b_dn, g_final, final, npt, tps):
    pos, te, nv, nxt = _routing_tables(ri, cnt, x.shape[0])
    yk = _gmm_call(pos, te, nv, nxt, h, w_gu, w_dn, b_gu, b_dn, layer)
    return _combine_call(yk, x, rg, mod3, layer, g_final, final, npt, tps)


def _rope_tables(n_tokens):
    n_rows = n_tokens // GRID_W
    rows = jnp.repeat(jnp.arange(n_rows, dtype=F32), GRID_W)
    cols = jnp.tile(jnp.arange(GRID_W, dtype=F32), n_rows)
    axis_dim = A_HEAD_DIM // 2
    inv = ROPE_THETA ** (-jnp.arange(0, axis_dim, 2, dtype=F32) / axis_dim)
    ang_r = rows[:, None] * inv[None, :]
    ang_c = cols[:, None] * inv[None, :]
    ang = jnp.concatenate([ang_r, ang_r, ang_c, ang_c], axis=-1)
    cos, sin = jnp.cos(ang), jnp.sin(ang)
    sign = jnp.where((jnp.arange(A_HEAD_DIM) % 32) < 16, -1.0, 1.0).astype(F32)
    sin = sin * sign[None, :]
    cos = jnp.concatenate([jnp.ones((TM, A_HEAD_DIM), F32), cos], axis=0)
    sin = jnp.concatenate([jnp.zeros((TM, A_HEAD_DIM), F32), sin], axis=0)
    return jnp.tile(cos, (1, LANES // A_HEAD_DIM)), jnp.tile(sin, (1, LANES // A_HEAD_DIM))


def kernel(x_prompt, x_sample, cache_diff_k, cache_diff_v, cache_gqa_k, cache_gqa_v, c, c_ctx, w_ada, b_ada, norm_mix, norm_ffn, norm_final, w_attn_in, w_attn_out, lam_q1, lam_k1, lam_q2, lam_k2, diff_subln, gqa_q_norm, gqa_k_norm, w_pool, pool_scale, w_router, b_router, w_gate_up, b_gate_up, w_down, b_down):
    n_prompt, prompt_len, d = x_prompt.shape
    n_sample, sample_len, _ = x_sample.shape
    assert d == D_MODEL and prompt_len % TM == 0 and sample_len % TM == 0
    assert n_sample + 1 <= SUBLANES
    npt = n_prompt * prompt_len // TM
    tps = sample_len // TM
    tp = n_prompt * prompt_len
    past = cache_diff_k.shape[2]

    xs = (x_prompt.reshape(tp, d), x_sample.reshape(n_sample * sample_len, d))
    cond8 = jnp.zeros((SUBLANES, d), F32).at[0].set(c_ctx).at[1:1 + n_sample].set(c)
    mod = _modulation_all(cond8, w_ada, b_ada)
    mod3 = mod[:, :1 + n_sample].reshape(DEPTH * (1 + n_sample), 1, N_MOD * d)
    assert n_sample == 2

    cos_t, sin_t = _rope_tables(sample_len)
    seg_r = jnp.arange(BQ_W)[:, None] // B_HEAD_DIM
    seg = (seg_r == seg_r.T).astype(BF16)

    caches = []
    y_final = None
    for i in range(DEPTH):
        j = i // 2
        g_mix = norm_mix[i][None]
        g_ffn = norm_ffn[i][None]
        wr = jnp.zeros((d, LANES), F32).at[:, :N_EXPERTS].set(w_router[i])
        br = jnp.full((1, LANES), NEG_BIG, F32).at[0, :N_EXPERTS].set(b_router[i])
        if i % 2 == 0:
            lambda_init = 0.8 - 0.6 * math.exp(-0.3 * i)
            gq = jnp.tile(gqa_q_norm[j], BQ_W // B_HEAD_DIM)[None]
            gk = jnp.tile(gqa_k_norm[j], BKV_W // B_HEAD_DIM)[None]
            qa, ka, va, qb, kbd, vb, ck, cv, cgk, cgv = _qkv_call(
                xs, mod3, i, g_mix, w_attn_in[j].astype(BF16), gq, gk, cos_t, sin_t, seg, npt, tps)
            caches.append((ck, cv, cgk, cgv))
            cka = cache_diff_k[:, j].reshape(n_sample, past, A_W).astype(BF16)
            cva = cache_diff_v[:, j].reshape(n_sample, past, A_W).astype(BF16)
            gk_c = cache_gqa_k[:, j]
            ckbd = jnp.concatenate([gk_c[:, :, 0], gk_c[:, :, 0], gk_c[:, :, 1], gk_c[:, :, 1]],
                                   axis=-1).astype(BF16)
            cvb = cache_gqa_v[:, j].reshape(n_sample, past, BKV_W).astype(BF16)
            o_p, o_s = _attention(lambda_init, qa, ka, va, qb, kbd, vb, cka, cva, ckbd, cvb,
                                  lam_q1[j][None], lam_k1[j][None], lam_q2[j][None], lam_k2[j][None],
                                  diff_subln[j][None], n_prompt, prompt_len, n_sample, sample_len)
            x, h, ri, rg, cnt = _post_attn_call(xs, o_p, o_s, mod3, i, w_attn_out[j].astype(BF16), g_ffn, wr, br,
                                                npt, tps)
        else:
            x, h, ri, rg, cnt = _pool_call(x, mod3, i, g_mix, w_pool[j].astype(BF16), pool_scale[j][None],
                                           g_ffn, wr, br, npt, tps)
        final = i == DEPTH - 1
        outs = _moe(x, h, ri, rg, cnt, mod3, i, w_gate_up, b_gate_up, w_down, b_down, norm_final[None], final, npt, tps)
        if final:
            y_final = outs
        else:
            x = outs[0]
            xs = (x,)

    y_prompt = y_final[0].reshape(n_prompt, prompt_len, d)
    y_sample = y_final[1].reshape(n_sample, sample_len, d)
    stack = lambda k, shp: jnp.stack([cc[k].reshape(shp) for cc in caches], axis=1)
    new_diff_k = stack(0, (n_prompt, prompt_len, A_HEADS, 2 * A_HEAD_DIM))
    new_diff_v = stack(1, (n_prompt, prompt_len, A_HEADS, 2 * A_HEAD_DIM))
    new_gqa_k = stack(2, (n_prompt, prompt_len, B_KV_HEADS, B_HEAD_DIM))
    new_gqa_v = stack(3, (n_prompt, prompt_len, B_KV_HEADS, B_HEAD_DIM))
    return (y_prompt, y_sample, new_diff_k, new_diff_v, new_gqa_k, new_gqa_v)
```

```python
import functools
import math

import jax
import jax.numpy as jnp
from jax import lax
from jax.experimental import pallas as pl
from jax.experimental.pallas import tpu as pltpu

F32 = jnp.float32
BF16 = jnp.bfloat16
I32 = jnp.int32

D_MODEL = 1024
DEPTH = 4
GRID_W = 64
A_HEADS = 4
A_HEAD_DIM = 64
B_Q_HEADS = 8
B_KV_HEADS = 2
B_HEAD_DIM = 64
ROPE_THETA = 10000.0
A_W = A_HEADS * 2 * A_HEAD_DIM
BQ_W = B_Q_HEADS * B_HEAD_DIM
BKV_W = B_KV_HEADS * B_HEAD_DIM
IN_W = 3 * A_W + BQ_W + 2 * BKV_W
POOL_WINDOWS = (2, 4, 8, 16)
POOL_GROUPS = 4
POOL_CH = D_MODEL // POOL_GROUPS
N_EXPERTS = 32
TOP_K = 4
D_EXPERT = D_MODEL
SWIGLU_LIMIT = 7.0
SWIGLU_ALPHA = 1.702
N_MOD = 6
EPS = 1e-6

LANES = 128
SUBLANES = 8
TM = 256
ROW_TILES = D_MODEL // LANES
HALO = 8
ADA_TN = 1536
VMEM_LIMIT = 56 * 1024 * 1024
NEG_BIG = -1e30


def _cparams(n_axes, vmem=VMEM_LIMIT):
    return pltpu.CompilerParams(
        dimension_semantics=("arbitrary",) * n_axes, vmem_limit_bytes=vmem)


def _dot(a, b):
    return jnp.dot(a, b, preferred_element_type=F32)


def _dot_nt(a, b):
    return lax.dot_general(a, b, (((1,), (1,)), ((), ())), preferred_element_type=F32)


def _rms(x, g):
    ms = jnp.mean(x * x, axis=-1, keepdims=True)
    return x * lax.rsqrt(ms + EPS) * g


def _lane_iota(shape):
    return lax.broadcasted_iota(I32, shape, len(shape) - 1)


def _store_token_tiles(ref, x):
    n = x.shape[0]
    for c in range(ROW_TILES):
        ref[pl.ds(c, n, stride=ROW_TILES), :] = x[:, c * LANES:(c + 1) * LANES]


def _load_token_tiles(ref, n):
    return jnp.concatenate([ref[pl.ds(c, n, stride=ROW_TILES), :] for c in range(ROW_TILES)], axis=1)


def _ada_kernel(c_ref, w_ref, b_ref, o_ref):
    c = c_ref[...]
    s = (c * jax.nn.sigmoid(c)).astype(BF16)
    o_ref[0] = _dot(s, w_ref[0].astype(BF16)) + b_ref[0]


def _modulation_all(cond8, w_ada, b_ada):
    nmod = w_ada.shape[-1]
    return pl.pallas_call(
        _ada_kernel,
        grid=(DEPTH, nmod // ADA_TN),
        in_specs=[
            pl.BlockSpec((SUBLANES, D_MODEL), lambda l, n: (0, 0)),
            pl.BlockSpec((1, D_MODEL, ADA_TN), lambda l, n: (l, 0, n)),
            pl.BlockSpec((1, 1, ADA_TN), lambda l, n: (l, 0, n)),
        ],
        out_specs=pl.BlockSpec((1, SUBLANES, ADA_TN), lambda l, n: (l, 0, n)),
        out_shape=jax.ShapeDtypeStruct((DEPTH, SUBLANES, nmod), F32),
        compiler_params=_cparams(2),
        name="modulation",
    )(cond8, w_ada, b_ada.reshape(DEPTH, 1, nmod))


def _seg_meansq(x, seg):
    x2 = x * x
    hi = x2.astype(BF16)
    lo = (x2 - hi.astype(F32)).astype(BF16)
    return (_dot(hi, seg) + _dot(lo, seg)) * (1.0 / B_HEAD_DIM)


def _rope128(x, cos, sin_signed):
    lo16 = (_lane_iota(x.shape) % 32) < 16
    nxt = pltpu.roll(x, LANES - 16, 1)
    prv = pltpu.roll(x, 16, 1)
    return x * cos + jnp.where(lo16, nxt, prv) * sin_signed


def _rope(x, cos, sin_signed):
    cols = [_rope128(x[:, c:c + LANES], cos, sin_signed) for c in range(0, x.shape[1], LANES)]
    return cols[0] if len(cols) == 1 else jnp.concatenate(cols, axis=1)


def _row_specs(xs, npt):
    if len(xs) == 1:
        return [pl.BlockSpec((TM, D_MODEL), lambda i: (i, 0))]
    return [pl.BlockSpec((TM, D_MODEL), lambda i: (jnp.minimum(i, npt - 1), 0)),
            pl.BlockSpec((TM, D_MODEL), lambda i: (jnp.maximum(i - npt, 0), 0))]


def _read_rows(i, npt, x_refs):
    if len(x_refs) == 1:
        return x_refs[0][...]
    return jnp.where(i < npt, x_refs[0][...], x_refs[1][...])


def _qkv_kernel(n_prompt_tiles, n_x, *refs):
    x_refs = refs[:n_x]
    (mod_ref, g_ref, w_ref, gq_ref, gk_ref, cos_ref, sin_ref, seg_ref,
     qa_ref, ka_ref, va_ref, qb_ref, kbd_ref, vb_ref, ck_ref, cv_ref, cgk_ref, cgv_ref) = refs[n_x:]
    i = pl.program_id(0)
    mod = mod_ref[0]
    shift = mod[:, 0:D_MODEL]
    scale = mod[:, D_MODEL:2 * D_MODEL]
    h = _rms(_read_rows(i, n_prompt_tiles, x_refs), g_ref[...]) * (1.0 + scale) + shift
    p = _dot(h.astype(BF16), w_ref[...])
    a_q = p[:, 0:A_W]
    a_k = p[:, A_W:2 * A_W]
    a_v = p[:, 2 * A_W:3 * A_W]
    o = 3 * A_W
    b_q = p[:, o:o + BQ_W]
    b_k = p[:, o + BQ_W:o + BQ_W + BKV_W]
    b_v = p[:, o + BQ_W + BKV_W:o + BQ_W + 2 * BKV_W]
    seg = seg_ref[...]
    b_q = b_q * lax.rsqrt(_seg_meansq(b_q, seg) + EPS) * gq_ref[...]
    b_k = b_k * lax.rsqrt(_seg_meansq(b_k, seg[:BKV_W, :BKV_W]) + EPS) * gk_ref[...]

    @pl.when(i < n_prompt_tiles)
    def _():
        ck_ref[...] = a_k
        cv_ref[...] = a_v
        cgk_ref[...] = b_k
        cgv_ref[...] = b_v

    cos = cos_ref[...]
    sin = sin_ref[...]
    sm = A_HEAD_DIM ** -0.5
    qa_ref[...] = (_rope(a_q, cos, sin) * sm).astype(BF16)
    ka_ref[...] = _rope(a_k, cos, sin).astype(BF16)
    va_ref[...] = a_v.astype(BF16)
    qb_ref[...] = (_rope(b_q, cos, sin) * (B_HEAD_DIM ** -0.5)).astype(BF16)
    kb = _rope(b_k, cos, sin)
    kb_sw = pltpu.roll(kb, B_HEAD_DIM, 1)
    lo = _lane_iota(kb.shape) < B_HEAD_DIM
    kbd_ref[...] = jnp.concatenate(
        [jnp.where(lo, kb, kb_sw), jnp.where(lo, kb_sw, kb)], axis=1).astype(BF16)
    vb_ref[...] = b_v.astype(BF16)


def _cond_row(i, npt, tiles_per_seq):
    return jnp.where(i < npt, 0, 1 + (i - npt) // tiles_per_seq)


def _qkv_call(xs, mod3, layer, g_mix, w_in_bf, gq, gk, cos_t, sin_t, seg, npt, tps):
    t = sum(x.shape[0] for x in xs)
    nt = t // TM
    tp = npt * TM
    row = lambda w: pl.BlockSpec((TM, w), lambda i: (i, 0))
    full = lambda a: pl.BlockSpec(a.shape, lambda i: (0,) * a.ndim)
    tab = pl.BlockSpec((TM, LANES), lambda i: (jnp.where(i < npt, 0, 1 + (i - npt) % tps), 0))
    cache = lambda w: pl.BlockSpec((TM, w), lambda i: (jnp.minimum(i, npt - 1), 0))
    return pl.pallas_call(
        functools.partial(_qkv_kernel, npt, len(xs)),
        grid=(nt,),
        in_specs=_row_specs(xs, npt) + [
            pl.BlockSpec((1, 1, N_MOD * D_MODEL), lambda i: (layer * 3 + _cond_row(i, npt, tps), 0, 0)),
            full(g_mix), full(w_in_bf), full(gq), full(gk), tab, tab, full(seg),
        ],
        out_specs=[row(A_W), row(A_W), row(A_W), row(BQ_W), row(2 * BKV_W), row(BKV_W),
                   cache(A_W), cache(A_W), cache(BKV_W), cache(BKV_W)],
        out_shape=[
            jax.ShapeDtypeStruct((t, A_W), BF16), jax.ShapeDtypeStruct((t, A_W), BF16),
            jax.ShapeDtypeStruct((t, A_W), BF16), jax.ShapeDtypeStruct((t, BQ_W), BF16),
            jax.ShapeDtypeStruct((t, 2 * BKV_W), BF16), jax.ShapeDtypeStruct((t, BKV_W), BF16),
            jax.ShapeDtypeStruct((tp, A_W), F32), jax.ShapeDtypeStruct((tp, A_W), F32),
            jax.ShapeDtypeStruct((tp, BKV_W), F32), jax.ShapeDtypeStruct((tp, BKV_W), F32),
        ],
        compiler_params=_cparams(1),
        name="attn_qkv",
    )(*xs, mod3, g_mix, w_in_bf, gq, gk, cos_t, sin_t, seg)


def _softmax_pv(qq, ks, vs):
    ss = [_dot_nt(qq, k) for k in ks]
    m = ss[0].max(axis=-1, keepdims=True)
    for s in ss[1:]:
        m = jnp.maximum(m, s.max(axis=-1, keepdims=True))
    acc = None
    l = None
    for s, v in zip(ss, vs):
        p = jnp.exp(s - m)
        ls = p.sum(axis=-1, keepdims=True)
        pv = _dot(p.astype(BF16), v)
        acc = pv if acc is None else acc + pv
        l = ls if l is None else l + ls
    return acc / l


def _split_halves(q):
    lo = _lane_iota(q.shape) < (LANES // 2)
    zero = jnp.zeros_like(q)
    return jnp.concatenate([jnp.where(lo, q, zero), jnp.where(lo, zero, q)], axis=0)


def _attn_body(lambda_init, qa, qb, ka_segs, va_segs, kbd_segs, vb_segs, lam, subln):
    tq = qa.shape[0]
    cols = []
    for h in range(A_HEADS):
        sl = slice(h * LANES, (h + 1) * LANES)
        o = _softmax_pv(_split_halves(qa[:, sl]), [k[:, sl] for k in ka_segs],
                        [v[:, sl] for v in va_segs])
        od = o[:tq] - lam * o[tq:]
        cols.append(_rms(od, subln) * (1.0 - lambda_init))
    lo = _lane_iota((tq, LANES)) < B_HEAD_DIM
    for c in range(BQ_W // LANES):
        g = (2 * c) // (B_Q_HEADS // B_KV_HEADS)
        sl = slice(c * LANES, (c + 1) * LANES)
        gl = slice(g * LANES, (g + 1) * LANES)
        o = _softmax_pv(_split_halves(qb[:, sl]), [k[:, gl] for k in kbd_segs], vb_segs)
        oe, oo = o[:tq], o[tq:]
        if g == 0:
            cols.append(jnp.where(lo, oe, pltpu.roll(oo, B_HEAD_DIM, 1)))
        else:
            cols.append(jnp.where(lo, pltpu.roll(oe, B_HEAD_DIM, 1), oo))
    return jnp.concatenate(cols, axis=1).astype(BF16)


def _lambda(lq1, lk1, lq2, lk2, lambda_init):
    return (jnp.exp(jnp.sum(lq1 * lk1, axis=-1, keepdims=True))
            - jnp.exp(jnp.sum(lq2 * lk2, axis=-1, keepdims=True)) + lambda_init)


def _attn_prompt_kernel(lambda_init, qa, ka, va, qb, kbd, vb, lq1, lk1, lq2, lk2, subln, o_ref):
    lam = _lambda(lq1[...], lk1[...], lq2[...], lk2[...], lambda_init)
    o_ref[...] = _attn_body(lambda_init, qa[...], qb[...], [ka[...]], [va[...]],
                            [kbd[...]], [vb[...]], lam, subln[...])


def _attn_sample_kernel(lambda_init, qa, ka, va, qb, kbd, vb, cka, cva, ckbd, cvb,
                        lq1, lk1, lq2, lk2, subln, o_ref):
    lam = _lambda(lq1[...], lk1[...], lq2[...], lk2[...], lambda_init)
    o_ref[...] = _attn_body(lambda_init, qa[...], qb[...], [ka[...], cka[...]], [va[...], cva[...]],
                            [kbd[...], ckbd[...]], [vb[...], cvb[...]], lam, subln[...])


def _attention(lambda_init, qa, ka, va, qb, kbd, vb, cka, cva, ckbd, cvb,
               lq1, lk1, lq2, lk2, subln, n_prompt, prompt_len, n_sample, sample_len):
    npt = n_prompt * prompt_len // TM
    small = [lq1, lk1, lq2, lk2, subln]
    widths = [A_W, A_W, A_W, BQ_W, 2 * BKV_W, BKV_W]
    assert prompt_len == TM
    full1 = lambda a: pl.BlockSpec(a.shape, lambda b: (0,) * a.ndim)
    o_p = pl.pallas_call(
        functools.partial(_attn_prompt_kernel, lambda_init),
        grid=(n_prompt,),
        in_specs=[pl.BlockSpec((TM, w), lambda b: (b, 0)) for w in widths] + [full1(a) for a in small],
        out_specs=pl.BlockSpec((TM, D_MODEL), lambda b: (b, 0)),
        out_shape=jax.ShapeDtypeStruct((npt * TM, D_MODEL), BF16),
        compiler_params=_cparams(1),
        name="attn_prompt",
    )(qa, ka, va, qb, kbd, vb, *small)
    tps = sample_len // TM
    seq0 = n_prompt * prompt_len // sample_len
    assert seq0 * sample_len == n_prompt * prompt_len
    past = cka.shape[1]
    qspec = lambda w: pl.BlockSpec((TM, w), lambda b, q: (npt + b * tps + q, 0))
    kspec = lambda w: pl.BlockSpec((sample_len, w), lambda b, q: (seq0 + b, 0))
    cspec = lambda w: pl.BlockSpec((None, past, w), lambda b, q: (b, 0, 0))
    full2 = lambda a: pl.BlockSpec(a.shape, lambda b, q: (0,) * a.ndim)
    o_s = pl.pallas_call(
        functools.partial(_attn_sample_kernel, lambda_init),
        grid=(n_sample, tps),
        in_specs=[qspec(A_W), kspec(A_W), kspec(A_W), qspec(BQ_W), kspec(2 * BKV_W), kspec(BKV_W),
                  cspec(A_W), cspec(A_W), cspec(2 * BKV_W), cspec(BKV_W)] + [full2(a) for a in small],
        out_specs=pl.BlockSpec((TM, D_MODEL), lambda b, q: (b * tps + q, 0)),
        out_shape=jax.ShapeDtypeStruct((n_sample * sample_len, D_MODEL), BF16),
        compiler_params=_cparams(2),
        name="attn_sample",
    )(qa, ka, va, qb, kbd, vb, cka, cva, ckbd, cvb, *small)
    return o_p, o_s


def _router_tail(i, x, y, mod, g_ffn, wr, br, xo_ref, h_ref, ri_ref, rg_ref, cnt_ref, carry_ref):
    gate_mix = mod[:, 2 * D_MODEL:3 * D_MODEL]
    shift = mod[:, 3 * D_MODEL:4 * D_MODEL]
    scale = mod[:, 4 * D_MODEL:5 * D_MODEL]
    xn = x + gate_mix * y
    xo_ref[...] = xn
    h = _rms(xn, g_ffn) * (1.0 + scale) + shift
    _store_token_tiles(h_ref, h)
    h_hi = h.astype(BF16)
    h_lo = (h - h_hi.astype(F32)).astype(BF16)
    w_hi = wr.astype(BF16)
    w_lo = (wr - w_hi.astype(F32)).astype(BF16)
    logits = _dot(h_hi, w_hi) + _dot(h_lo, w_hi) + _dot(h_hi, w_lo) + br
    lane = _lane_iota(logits.shape)
    lane_f = lane.astype(F32)
    vals, idxs = [], []
    l = logits
    for _ in range(TOP_K):
        m = l.max(axis=-1, keepdims=True)
        idx = jnp.where(l == m, lane_f, float(LANES)).min(axis=-1, keepdims=True)
        vals.append(m)
        idxs.append(idx)
        l = jnp.where(lane_f == idx, -jnp.inf, l)
    es = [jnp.exp(v - vals[0]) for v in vals]
    den = es[0]
    for e in es[1:]:
        den = den + e
    sel = jnp.zeros(logits.shape, F32)
    for idx in idxs:
        sel = sel + jnp.where(lane_f == idx, 1.0, 0.0)

    @pl.when(i == 0)
    def _():
        carry_ref[...] = jnp.zeros_like(carry_ref)

    carry = carry_ref[...]
    r_io = lax.broadcasted_iota(I32, (TM, TM), 0)
    c_io = lax.broadcasted_iota(I32, (TM, TM), 1)
    tri = jnp.where(c_io < r_io, 1.0, 0.0).astype(BF16)
    rank = _dot(tri, sel.astype(BF16)) + carry
    carry = carry + sel.sum(axis=0, keepdims=True)
    carry_ref[...] = carry
    cnt_ref[...] = carry
    ri = jnp.zeros(logits.shape, F32)
    rg = jnp.zeros(logits.shape, F32)
    for k in range(TOP_K):
        rk = jnp.where(lane_f == idxs[k], rank, 0.0).sum(axis=-1, keepdims=True)
        ri = jnp.where(lane == k, idxs[k], ri)
        ri = jnp.where(lane == TOP_K + k, rk, ri)
        rg = jnp.where(lane == k, es[k] / den, rg)
    ri_ref[...] = ri.astype(I32)
    rg_ref[...] = rg


def _tail_specs(t):
    row = lambda w: pl.BlockSpec((TM, w), lambda i: (i, 0))
    out_specs = [row(D_MODEL), pl.BlockSpec((TM * ROW_TILES, LANES), lambda i: (i, 0)), row(LANES), row(LANES),
                 pl.BlockSpec((1, LANES), lambda i: (0, 0))]
    out_shape = [jax.ShapeDtypeStruct((t, D_MODEL), F32), jax.ShapeDtypeStruct((t * ROW_TILES, LANES), F32),
                 jax.ShapeDtypeStruct((t, LANES), I32), jax.ShapeDtypeStruct((t, LANES), F32),
                 jax.ShapeDtypeStruct((1, LANES), F32)]
    return out_specs, out_shape


def _post_attn_kernel(npt, n_x, *refs):
    x_refs = refs[:n_x]
    (op_ref, os_ref, mod_ref, w_ref, g_ref, wr_ref, br_ref,
     xo_ref, h_ref, ri_ref, rg_ref, cnt_ref, carry_ref) = refs[n_x:]
    i = pl.program_id(0)
    y = _dot(jnp.where(i < npt, op_ref[...], os_ref[...]), w_ref[...])
    _router_tail(i, _read_rows(i, npt, x_refs), y, mod_ref[0], g_ref[...], wr_ref[...], br_ref[...],
                 xo_ref, h_ref, ri_ref, rg_ref, cnt_ref, carry_ref)


def _post_attn_call(xs, o_p, o_s, mod3, layer, w_out_bf, g_ffn, wr, br, npt, tps):
    t = sum(x.shape[0] for x in xs)
    full = lambda a: pl.BlockSpec(a.shape, lambda i: (0,) * a.ndim)
    out_specs, out_shape = _tail_specs(t)
    return pl.pallas_call(
        functools.partial(_post_attn_kernel, npt, len(xs)),
        grid=(t // TM,),
        in_specs=_row_specs(xs, npt) + [
                  pl.BlockSpec((TM, D_MODEL), lambda i: (jnp.minimum(i, npt - 1), 0)),
                  pl.BlockSpec((TM, D_MODEL), lambda i: (jnp.maximum(i - npt, 0), 0)),
                  pl.BlockSpec((1, 1, N_MOD * D_MODEL), lambda i: (layer * 3 + _cond_row(i, npt, tps), 0, 0)),
                  full(w_out_bf), full(g_ffn), full(wr), full(br)],
        out_specs=out_specs, out_shape=out_shape,
        scratch_shapes=[pltpu.VMEM((1, LANES), F32)],
        compiler_params=_cparams(1),
        name="attn_out_router",
    )(*xs, o_p, o_s, mod3, w_out_bf, g_ffn, wr, br)


def _pool_kernel(npt, tps, x_ref, xp_ref, xn_ref, mod_ref, gm_ref, wp_ref, ps_ref, g_ref, wr_ref, br_ref,
                 xo_ref, h_ref, ri_ref, rg_ref, cnt_ref, carry_ref):
    i = pl.program_id(0)
    mod = mod_ref[0]
    shift = mod[:, 0:D_MODEL]
    scale = mod[:, D_MODEL:2 * D_MODEL]
    gm = gm_ref[...]
    x = x_ref[...]
    pre = lambda v: _rms(v, gm) * (1.0 + scale) + shift
    j = jnp.where(i < npt, 0, (i - npt) % tps)
    ntile = jnp.where(i < npt, 1, tps)
    has_prev = (j > 0).astype(F32)
    has_next = (j < ntile - 1).astype(F32)
    h = pre(x)
    hc = jnp.concatenate([pre(xp_ref[...]) * has_prev, h, pre(xn_ref[...]) * has_next], axis=0)
    rows = hc.shape[0]
    t_seq = (j * TM + lax.broadcasted_iota(I32, (TM, 1), 0)).astype(F32)
    seq_len = (ntile * TM).astype(F32)
    ys = []
    for g, w in enumerate(POOL_WINDOWS):
        half = w // 2
        s = hc[:, g * POOL_CH:(g + 1) * POOL_CH]
        step = 1
        while step < w:
            s = s + pltpu.roll(s, rows - step, 0)
            step *= 2
        s = pltpu.roll(s, half, 0) if half != HALO else s
        win = s[HALO:HALO + TM] if half != HALO else s[0:TM]
        cnt = jnp.minimum(t_seq + half, seq_len) - jnp.maximum(t_seq - half, 0.0)
        pooled = win / cnt - h[:, g * POOL_CH:(g + 1) * POOL_CH]
        ys.append(_dot(pooled.astype(BF16), wp_ref[g]))
    y = jnp.concatenate(ys, axis=1) * ps_ref[...]
    _router_tail(i, x, y, mod, g_ref[...], wr_ref[...], br_ref[...],
                 xo_ref, h_ref, ri_ref, rg_ref, cnt_ref, carry_ref)


def _pool_call(x, mod3, layer, g_mix, w_pool_bf, pool_scale, g_ffn, wr, br, npt, tps):
    t = x.shape[0]
    per = TM // HALO
    nh = t // HALO
    row = lambda w: pl.BlockSpec((TM, w), lambda i: (i, 0))
    full = lambda a: pl.BlockSpec(a.shape, lambda i: (0,) * a.ndim)
    out_specs, out_shape = _tail_specs(t)
    return pl.pallas_call(
        functools.partial(_pool_kernel, npt, tps),
        grid=(t // TM,),
        in_specs=[row(D_MODEL),
                  pl.BlockSpec((HALO, D_MODEL), lambda i: (jnp.maximum(i * per - 1, 0), 0)),
                  pl.BlockSpec((HALO, D_MODEL), lambda i: (jnp.minimum((i + 1) * per, nh - 1), 0)),
                  pl.BlockSpec((1, 1, N_MOD * D_MODEL), lambda i: (layer * 3 + _cond_row(i, npt, tps), 0, 0)),
                  full(g_mix), full(w_pool_bf), full(pool_scale), full(g_ffn), full(wr), full(br)],
        out_specs=out_specs, out_shape=out_shape,
        scratch_shapes=[pltpu.VMEM((1, LANES), F32)],
        compiler_params=_cparams(1),
        name="pool_router",
    )(x, x, x, mod3, g_mix, w_pool_bf, pool_scale, g_ffn, wr, br)


def _expert_rows(x, e, wgu_bf, wd_bf, bgu_ref, bd_ref):
    gu = _dot(x, wgu_bf[...]) + bgu_ref[pl.ds(e, 1), :]
    gate = jnp.minimum(gu[:, :D_EXPERT], SWIGLU_LIMIT)
    up = jnp.clip(gu[:, D_EXPERT:], -SWIGLU_LIMIT, SWIGLU_LIMIT)
    act = gate * jax.nn.sigmoid(SWIGLU_ALPHA * gate) * (up + 1.0)
    return _dot(act.astype(BF16), wd_bf[...]) + bd_ref[pl.ds(e, 1), :]


ROW_GROUP = 16


def _for_rows(n, fn):
    def group(g, c):
        for q in range(ROW_GROUP):
            fn(g * ROW_GROUP + q, q % 2)
        return c

    def single(r, c):
        fn(r, 0)
        return c
    full = n // ROW_GROUP
    lax.fori_loop(0, full, group, 0)
    lax.fori_loop(full * ROW_GROUP, n, single, 0)


def _gmm_kernel(layer, n_tok, pos_ref, te_ref, nv_ref, nxt_ref,
                h_hbm, wgu_hbm, wd_hbm, bgu_ref, bd_ref, yk_hbm,
                src_ref, cur_ref, xbuf, ybuf, wgu_st, wd_st, wgu_bf, wd_bf, sem_g, sem_s, sem_w):
    w = pl.program_id(0)
    n_steps = pl.num_programs(0)
    slot = w % 2
    nv = nv_ref[w]
    nv_next = jnp.where(w + 1 < n_steps, nv_ref[jnp.minimum(w + 1, n_steps - 1)], 0)

    def weights_copy(e):
        return (pltpu.make_async_copy(wgu_hbm.at[layer, e], wgu_st, sem_w.at[0]),
                pltpu.make_async_copy(wd_hbm.at[layer, e], wd_st, sem_w.at[1]))

    def tile_of(ref, start):
        return ref.at[pl.ds(pl.multiple_of(start, ROW_TILES), ROW_TILES), :]

    def rows_of(ref, n):
        return ref.at[pl.ds(0, pl.multiple_of(n * ROW_TILES, ROW_TILES)), :]

    def gather_start(tile, dst_slot, n):
        def one(r, prio):
            tok_row = src_ref[tile * TM + r] & (n_tok * ROW_TILES - 1)
            pltpu.make_async_copy(tile_of(h_hbm, tok_row), tile_of(xbuf.at[dst_slot], r * ROW_TILES),
                                  sem_g.at[dst_slot]).start(priority=prio)
        _for_rows(n, one)

    def gather_wait(dst_slot, n):
        pltpu.make_async_copy(rows_of(h_hbm, n), rows_of(xbuf.at[dst_slot], n), sem_g.at[dst_slot]).wait()

    def scatter_start(tile, src_slot, n):
        def one(r, prio):
            pltpu.make_async_copy(tile_of(ybuf.at[src_slot], r * ROW_TILES),
                                  tile_of(yk_hbm, src_ref[tile * TM + r]),
                                  sem_s.at[src_slot]).start(priority=prio)
        _for_rows(n, one)

    def scatter_wait(src_slot, n):
        pltpu.make_async_copy(rows_of(ybuf.at[src_slot], n), rows_of(yk_hbm, n), sem_s.at[src_slot]).wait()

    @pl.when(w == 0)
    def _():
        def invert(a, c):
            src_ref[pos_ref[a]] = a * ROW_TILES
            return c
        lax.fori_loop(0, n_tok * TOP_K, invert, 0, unroll=8)
        xbuf[...] = jnp.zeros_like(xbuf)
        cur_ref[0] = -1
        for cp in weights_copy(te_ref[0]):
            cp.start()
        gather_start(0, 0, nv)

    @pl.when(nv > 0)
    def _():
        e = te_ref[w]
        gather_wait(slot, nv)

        @pl.when(nv_next > 0)
        def _():
            gather_start(w + 1, 1 - slot, nv_next)

        @pl.when(w >= 2)
        def _():
            scatter_wait(slot, nv_ref[jnp.maximum(w - 2, 0)])

        @pl.when(cur_ref[0] != e)
        def _():
            for cp in weights_copy(e):
                cp.wait()
            wgu_bf[...] = wgu_st[...].astype(BF16)
            wd_bf[...] = wd_st[...].astype(BF16)
            cur_ref[0] = e

            @pl.when(nxt_ref[e] < N_EXPERTS)
            def _():
                for cp in weights_copy(nxt_ref[e]):
                    cp.start()

        def run(rows):
            x = _load_token_tiles(xbuf.at[slot], rows).astype(BF16)
            _store_token_tiles(ybuf.at[slot], _expert_rows(x, e, wgu_bf, wd_bf, bgu_ref, bd_ref))

        @pl.when(nv > TM // 2)
        def _():
            run(TM)

        @pl.when(nv <= TM // 2)
        def _():
            run(TM // 2)

        scatter_start(w, slot, nv)

        @pl.when(nv_next == 0)
        def _():
            @pl.when(w >= 1)
            def _():
                scatter_wait(1 - slot, nv_ref[jnp.maximum(w - 1, 0)])
            scatter_wait(slot, nv)


def _gmm_call(pos, te, nv, nxt, h, w_gu, w_dn, b_gu, b_dn, layer):
    assert ROW_TILES == SUBLANES
    t = h.shape[0] // ROW_TILES
    assert t & (t - 1) == 0
    n_rows = t * TOP_K
    n_steps = te.shape[0]
    return pl.pallas_call(
        functools.partial(_gmm_kernel, layer, t),
        grid_spec=pltpu.PrefetchScalarGridSpec(
            num_scalar_prefetch=4, grid=(n_steps,),
            in_specs=[pl.BlockSpec(memory_space=pl.ANY), pl.BlockSpec(memory_space=pl.ANY),
                      pl.BlockSpec(memory_space=pl.ANY),
                      pl.BlockSpec((None, N_EXPERTS, 2 * D_EXPERT), lambda w, *_: (layer, 0, 0)),
                      pl.BlockSpec((None, N_EXPERTS, D_MODEL), lambda w, *_: (layer, 0, 0))],
            out_specs=pl.BlockSpec(memory_space=pl.ANY),
            scratch_shapes=[
                pltpu.SMEM((n_steps * TM,), I32), pltpu.SMEM((1,), I32),
                pltpu.VMEM((2, TM * ROW_TILES, LANES), F32), pltpu.VMEM((2, TM * ROW_TILES, LANES), F32),
                pltpu.VMEM((D_MODEL, 2 * D_EXPERT), F32), pltpu.VMEM((D_EXPERT, D_MODEL), F32),
                pltpu.VMEM((D_MODEL, 2 * D_EXPERT), BF16), pltpu.VMEM((D_EXPERT, D_MODEL), BF16),
                pltpu.SemaphoreType.DMA((2,)), pltpu.SemaphoreType.DMA((2,)), pltpu.SemaphoreType.DMA((2,))]),
        out_shape=jax.ShapeDtypeStruct((n_rows * ROW_TILES, LANES), F32),
        compiler_params=_cparams(1),
        name="moe_experts",
    )(pos, te, nv, nxt, h, w_gu, w_dn, b_gu, b_dn)


def _combine_kernel(final, npt, x_ref, rg_ref, mod_ref, gf_ref, y0_ref, y1_ref, y2_ref, y3_ref, *outs):
    i = pl.program_id(0)
    rg = rg_ref[...]
    moe = rg[:, 0:1] * _load_token_tiles(y0_ref, TM)
    for k, y_ref in enumerate((y1_ref, y2_ref, y3_ref), start=1):
        moe = moe + rg[:, k:k + 1] * _load_token_tiles(y_ref, TM)
    gate_ffn = mod_ref[0][:, 5 * D_MODEL:6 * D_MODEL]
    xn = x_ref[...] + gate_ffn * moe
    if not final:
        outs[0][...] = xn
        return
    y = _rms(xn, gf_ref[...])
    yp_ref, ys_ref = outs

    @pl.when(i < npt)
    def _():
        yp_ref[...] = y

    @pl.when(i >= npt)
    def _():
        ys_ref[...] = y


def _combine_call(yk, x, rg, mod3, layer, g_final, final, npt, tps):
    t = x.shape[0]
    nt = t // TM
    assert TOP_K == 4
    row = lambda w: pl.BlockSpec((TM, w), lambda i: (i, 0))
    slab = lambda k: pl.BlockSpec((TM * ROW_TILES, LANES), lambda i: (k * nt + i, 0))
    if final:
        out_specs = [pl.BlockSpec((TM, D_MODEL), lambda i: (jnp.minimum(i, npt - 1), 0)),
                     pl.BlockSpec((TM, D_MODEL), lambda i: (jnp.maximum(i - npt, 0), 0))]
        out_shape = [jax.ShapeDtypeStruct((npt * TM, D_MODEL), F32),
                     jax.ShapeDtypeStruct(((nt - npt) * TM, D_MODEL), F32)]
    else:
        out_specs = [row(D_MODEL)]
        out_shape = [jax.ShapeDtypeStruct((t, D_MODEL), F32)]
    return pl.pallas_call(
        functools.partial(_combine_kernel, final, npt),
        grid=(nt,),
        in_specs=[row(D_MODEL), row(LANES),
                  pl.BlockSpec((1, 1, N_MOD * D_MODEL), lambda i: (layer * 3 + _cond_row(i, npt, tps), 0, 0)),
                  pl.BlockSpec(g_final.shape, lambda i: (0, 0))] + [slab(k) for k in range(TOP_K)],
        out_specs=out_specs, out_shape=out_shape,
        compiler_params=_cparams(1),
        name="moe_combine",
    )(x, rg, mod3, g_final, yk, yk, yk, yk)


def _routing_tables(ri, cnt, n_tok):
    ex = jnp.arange(N_EXPERTS, dtype=I32)
    counts = cnt[0, :N_EXPERTS].astype(I32)
    tiles = (counts + (TM - 1)) // TM
    tile_end = jnp.sum(jnp.where(ex[None, :] <= ex[:, None], tiles[None, :], 0), axis=1)
    tile_start = tile_end - tiles
    e_idx = ri[:, :TOP_K]
    rank = ri[:, TOP_K:2 * TOP_K]
    onehot = e_idx[:, :, None] == ex[None, None, :]
    pos = rank + jnp.sum(jnp.where(onehot, tile_start[None, None, :] * TM, 0), axis=-1)
    pos = pos.T.reshape(-1).astype(I32)
    n_steps = n_tok * TOP_K // TM + N_EXPERTS
    w = jnp.arange(n_steps, dtype=I32)
    te = jnp.minimum(jnp.sum((tile_end[None, :] <= w[:, None]).astype(I32), axis=1), N_EXPERTS - 1)
    mine = te[:, None] == ex[None, :]
    left = jnp.sum(jnp.where(mine, counts[None, :] - (w[:, None] - tile_start[None, :]) * TM, 0), axis=1)
    nv = jnp.where(w < tile_end[-1], jnp.clip(left, 0, TM), 0)
    later = (ex[None, :] > ex[:, None]) & (counts[None, :] > 0)
    nxt = jnp.min(jnp.where(later, ex[None, :], N_EXPERTS), axis=1)
    return pos, te.astype(I32), nv.astype(I32), nxt.astype(I32)


def _moe(x, h, ri, rg, cnt, mod3, layer, w_gu, b_gu, w_dn, b_dn, g_final, final, npt, tps):
    pos, te, nv, nxt = _routing_tables(ri, cnt, x.shape[0])
    yk = _gmm_call(pos, te, nv, nxt, h, w_gu, w_dn, b_gu, b_dn, layer)
    return _combine_call(yk, x, rg, mod3, layer, g_final, final, npt, tps)


def _rope_tables(n_tokens):
    n_rows = n_tokens // GRID_W
    rows = jnp.repeat(jnp.arange(n_rows, dtype=F32), GRID_W)
    cols = jnp.tile(jnp.arange(GRID_W, dtype=F32), n_rows)
    axis_dim = A_HEAD_DIM // 2
    inv = ROPE_THETA ** (-jnp.arange(0, axis_dim, 2, dtype=F32) / axis_dim)
    ang_r = rows[:, None] * inv[None, :]
    ang_c = cols[:, None] * inv[None, :]
    ang = jnp.concatenate([ang_r, ang_r, ang_c, ang_c], axis=-1)
    cos, sin = jnp.cos(ang), jnp.sin(ang)
    sign = jnp.where((jnp.arange(A_HEAD_DIM) % 32) < 16, -1.0, 1.0).astype(F32)
    sin = sin * sign[None, :]
    cos = jnp.concatenate([jnp.ones((TM, A_HEAD_DIM), F32), cos], axis=0)
    sin = jnp.concatenate([jnp.zeros((TM, A_HEAD_DIM), F32), sin], axis=0)
    return jnp.tile(cos, (1, LANES // A_HEAD_DIM)), jnp.tile(sin, (1, LANES // A_HEAD_DIM))


def kernel(x_prompt, x_sample, cache_diff_k, cache_diff_v, cache_gqa_k, cache_gqa_v, c, c_ctx, w_ada, b_ada, norm_mix, norm_ffn, norm_final, w_attn_in, w_attn_out, lam_q1, lam_k1, lam_q2, lam_k2, diff_subln, gqa_q_norm, gqa_k_norm, w_pool, pool_scale, w_router, b_router, w_gate_up, b_gate_up, w_down, b_down):
    n_prompt, prompt_len, d = x_prompt.shape
    n_sample, sample_len, _ = x_sample.shape
    assert d == D_MODEL and prompt_len % TM == 0 and sample_len % TM == 0
    assert n_sample + 1 <= SUBLANES
    npt = n_prompt * prompt_len // TM
    tps = sample_len // TM
    tp = n_prompt * prompt_len
    past = cache_diff_k.shape[2]

    xs = (x_prompt.reshape(tp, d), x_sample.reshape(n_sample * sample_len, d))
    cond8 = jnp.zeros((SUBLANES, d), F32).at[0].set(c_ctx).at[1:1 + n_sample].set(c)
    mod = _modulation_all(cond8, w_ada, b_ada)
    mod3 = mod[:, :1 + n_sample].reshape(DEPTH * (1 + n_sample), 1, N_MOD * d)
    assert n_sample == 2

    cos_t, sin_t = _rope_tables(sample_len)
    seg_r = jnp.arange(BQ_W)[:, None] // B_HEAD_DIM
    seg = (seg_r == seg_r.T).astype(BF16)

    caches = []
    y_final = None
    for i in range(DEPTH):
        j = i // 2
        g_mix = norm_mix[i][None]
        g_ffn = norm_ffn[i][None]
        wr = jnp.zeros((d, LANES), F32).at[:, :N_EXPERTS].set(w_router[i])
        br = jnp.full((1, LANES), NEG_BIG, F32).at[0, :N_EXPERTS].set(b_router[i])
        if i % 2 == 0:
            lambda_init = 0.8 - 0.6 * math.exp(-0.3 * i)
            gq = jnp.tile(gqa_q_norm[j], BQ_W // B_HEAD_DIM)[None]
            gk = jnp.tile(gqa_k_norm[j], BKV_W // B_HEAD_DIM)[None]
            qa, ka, va, qb, kbd, vb, ck, cv, cgk, cgv = _qkv_call(
                xs, mod3, i, g_mix, w_attn_in[j].astype(BF16), gq, gk, cos_t, sin_t, seg, npt, tps)
            caches.append((ck, cv, cgk, cgv))
            cka = cache_diff_k[:, j].reshape(n_sample, past, A_W).astype(BF16)
            cva = cache_diff_v[:, j].reshape(n_sample, past, A_W).astype(BF16)
            gk_c = cache_gqa_k[:, j]
            ckbd = jnp.concatenate([gk_c[:, :, 0], gk_c[:, :, 0], gk_c[:, :, 1], gk_c[:, :, 1]],
                                   axis=-1).astype(BF16)
            cvb = cache_gqa_v[:, j].reshape(n_sample, past, BKV_W).astype(BF16)
            o_p, o_s = _attention(lambda_init, qa, ka, va, qb, kbd, vb, cka, cva, ckbd, cvb,
                                  lam_q1[j][None], lam_k1[j][None], lam_q2[j][None], lam_k2[j][None],
                                  diff_subln[j][None], n_prompt, prompt_len, n_sample, sample_len)
            x, h, ri, rg, cnt = _post_attn_call(xs, o_p, o_s, mod3, i, w_attn_out[j].astype(BF16), g_ffn, wr, br,
                                                npt, tps)
        else:
            x, h, ri, rg, cnt = _pool_call(x, mod3, i, g_mix, w_pool[j].astype(BF16), pool_scale[j][None],
                                           g_ffn, wr, br, npt, tps)
        final = i == DEPTH - 1
        outs = _moe(x, h, ri, rg, cnt, mod3, i, w_gate_up, b_gate_up, w_down, b_down, norm_final[None], final, npt, tps)
        if final:
            y_final = outs
        else:
            x = outs[0]
            xs = (x,)

    y_prompt = y_final[0].reshape(n_prompt, prompt_len, d)
    y_sample = y_final[1].reshape(n_sample, sample_len, d)
    stack = lambda k, shp: jnp.stack([cc[k].reshape(shp) for cc in caches], axis=1)
    new_diff_k = stack(0, (n_prompt, prompt_len, A_HEADS, 2 * A_HEAD_DIM))
    new_diff_v = stack(1, (n_prompt, prompt_len, A_HEADS, 2 * A_HEAD_DIM))
    new_gqa_k = stack(2, (n_prompt, prompt_len, B_KV_HEADS, B_HEAD_DIM))
    new_gqa_v = stack(3, (n_prompt, prompt_len, B_KV_HEADS, B_HEAD_DIM))
    return (y_prompt, y_sample, new_diff_k, new_diff_v, new_gqa_k, new_gqa_v)
```

```python
import functools
import math

import jax
import jax.numpy as jnp
from jax import lax
from jax.experimental import pallas as pl
from jax.experimental.pallas import tpu as pltpu

F32 = jnp.float32
BF16 = jnp.bfloat16
I32 = jnp.int32

D_MODEL = 1024
DEPTH = 4
GRID_W = 64
A_HEADS = 4
A_HEAD_DIM = 64
B_Q_HEADS = 8
B_KV_HEADS = 2
B_HEAD_DIM = 64
ROPE_THETA = 10000.0
A_W = A_HEADS * 2 * A_HEAD_DIM
BQ_W = B_Q_HEADS * B_HEAD_DIM
BKV_W = B_KV_HEADS * B_HEAD_DIM
IN_W = 3 * A_W + BQ_W + 2 * BKV_W
POOL_WINDOWS = (2, 4, 8, 16)
POOL_GROUPS = 4
POOL_CH = D_MODEL // POOL_GROUPS
N_EXPERTS = 32
TOP_K = 4
D_EXPERT = D_MODEL
SWIGLU_LIMIT = 7.0
SWIGLU_ALPHA = 1.702
N_MOD = 6
EPS = 1e-6

LANES = 128
SUBLANES = 8
TM = 256
ROW_TILES = D_MODEL // LANES
HALO = 8
ADA_TN = 1536
VMEM_LIMIT = 56 * 1024 * 1024
NEG_BIG = -1e30


def _cparams(n_axes, vmem=VMEM_LIMIT):
    return pltpu.CompilerParams(
        dimension_semantics=("arbitrary",) * n_axes, vmem_limit_bytes=vmem)


def _dot(a, b):
    return jnp.dot(a, b, preferred_element_type=F32)


def _dot_nt(a, b):
    return lax.dot_general(a, b, (((1,), (1,)), ((), ())), preferred_element_type=F32)


def _rms(x, g):
    ms = jnp.mean(x * x, axis=-1, keepdims=True)
    return x * lax.rsqrt(ms + EPS) * g


def _lane_iota(shape):
    return lax.broadcasted_iota(I32, shape, len(shape) - 1)


def _store_token_tiles(ref, x):
    n = x.shape[0]
    for c in range(ROW_TILES):
        ref[pl.ds(c, n, stride=ROW_TILES), :] = x[:, c * LANES:(c + 1) * LANES]


def _load_token_tiles(ref, n):
    return jnp.concatenate([ref[pl.ds(c, n, stride=ROW_TILES), :] for c in range(ROW_TILES)], axis=1)


def _ada_kernel(c_ref, w_ref, b_ref, o_ref):
    c = c_ref[...]
    s = (c * jax.nn.sigmoid(c)).astype(BF16)
    o_ref[0] = _dot(s, w_ref[0].astype(BF16)) + b_ref[0]


def _modulation_all(cond8, w_ada, b_ada):
    nmod = w_ada.shape[-1]
    return pl.pallas_call(
        _ada_kernel,
        grid=(DEPTH, nmod // ADA_TN),
        in_specs=[
            pl.BlockSpec((SUBLANES, D_MODEL), lambda l, n: (0, 0)),
            pl.BlockSpec((1, D_MODEL, ADA_TN), lambda l, n: (l, 0, n)),
            pl.BlockSpec((1, 1, ADA_TN), lambda l, n: (l, 0, n)),
        ],
        out_specs=pl.BlockSpec((1, SUBLANES, ADA_TN), lambda l, n: (l, 0, n)),
        out_shape=jax.ShapeDtypeStruct((DEPTH, SUBLANES, nmod), F32),
        compiler_params=_cparams(2),
        name="modulation",
    )(cond8, w_ada, b_ada.reshape(DEPTH, 1, nmod))


def _seg_meansq(x, seg):
    x2 = x * x
    hi = x2.astype(BF16)
    lo = (x2 - hi.astype(F32)).astype(BF16)
    return (_dot(hi, seg) + _dot(lo, seg)) * (1.0 / B_HEAD_DIM)


def _rope128(x, cos, sin_signed):
    lo16 = (_lane_iota(x.shape) % 32) < 16
    nxt = pltpu.roll(x, LANES - 16, 1)
    prv = pltpu.roll(x, 16, 1)
    return x * cos + jnp.where(lo16, nxt, prv) * sin_signed


def _rope(x, cos, sin_signed):
    cols = [_rope128(x[:, c:c + LANES], cos, sin_signed) for c in range(0, x.shape[1], LANES)]
    return cols[0] if len(cols) == 1 else jnp.concatenate(cols, axis=1)


def _row_specs(xs, npt):
    if len(xs) == 1:
        return [pl.BlockSpec((TM, D_MODEL), lambda i: (i, 0))]
    return [pl.BlockSpec((TM, D_MODEL), lambda i: (jnp.minimum(i, npt - 1), 0)),
            pl.BlockSpec((TM, D_MODEL), lambda i: (jnp.maximum(i - npt, 0), 0))]


def _read_rows(i, npt, x_refs):
    if len(x_refs) == 1:
        return x_refs[0][...]
    return jnp.where(i < npt, x_refs[0][...], x_refs[1][...])


def _qkv_kernel(n_prompt_tiles, n_x, *refs):
    x_refs = refs[:n_x]
    (mod_ref, g_ref, w_ref, gq_ref, gk_ref, cos_ref, sin_ref, seg_ref,
     qa_ref, ka_ref, va_ref, qb_ref, kbd_ref, vb_ref, ck_ref, cv_ref, cgk_ref, cgv_ref) = refs[n_x:]
    i = pl.program_id(0)
    mod = mod_ref[0]
    shift = mod[:, 0:D_MODEL]
    scale = mod[:, D_MODEL:2 * D_MODEL]
    h = _rms(_read_rows(i, n_prompt_tiles, x_refs), g_ref[...]) * (1.0 + scale) + shift
    p = _dot(h.astype(BF16), w_ref[...])
    a_q = p[:, 0:A_W]
    a_k = p[:, A_W:2 * A_W]
    a_v = p[:, 2 * A_W:3 * A_W]
    o = 3 * A_W
    b_q = p[:, o:o + BQ_W]
    b_k = p[:, o + BQ_W:o + BQ_W + BKV_W]
    b_v = p[:, o + BQ_W + BKV_W:o + BQ_W + 2 * BKV_W]
    seg = seg_ref[...]
    b_q = b_q * lax.rsqrt(_seg_meansq(b_q, seg) + EPS) * gq_ref[...]
    b_k = b_k * lax.rsqrt(_seg_meansq(b_k, seg[:BKV_W, :BKV_W]) + EPS) * gk_ref[...]

    @pl.when(i < n_prompt_tiles)
    def _():
        ck_ref[...] = a_k
        cv_ref[...] = a_v
        cgk_ref[...] = b_k
        cgv_ref[...] = b_v

    cos = cos_ref[...]
    sin = sin_ref[...]
    sm = A_HEAD_DIM ** -0.5
    qa_ref[...] = (_rope(a_q, cos, sin) * sm).astype(BF16)
    ka_ref[...] = _rope(a_k, cos, sin).astype(BF16)
    va_ref[...] = a_v.astype(BF16)
    qb_ref[...] = (_rope(b_q, cos, sin) * (B_HEAD_DIM ** -0.5)).astype(BF16)
    kb = _rope(b_k, cos, sin)
    kb_sw = pltpu.roll(kb, B_HEAD_DIM, 1)
    lo = _lane_iota(kb.shape) < B_HEAD_DIM
    kbd_ref[...] = jnp.concatenate(
        [jnp.where(lo, kb, kb_sw), jnp.where(lo, kb_sw, kb)], axis=1).astype(BF16)
    vb_ref[...] = b_v.astype(BF16)


def _cond_row(i, npt, tiles_per_seq):
    return jnp.where(i < npt, 0, 1 + (i - npt) // tiles_per_seq)


def _qkv_call(xs, mod3, layer, g_mix, w_in_bf, gq, gk, cos_t, sin_t, seg, npt, tps):
    t = sum(x.shape[0] for x in xs)
    nt = t // TM
    tp = npt * TM
    row = lambda w: pl.BlockSpec((TM, w), lambda i: (i, 0))
    full = lambda a: pl.BlockSpec(a.shape, lambda i: (0,) * a.ndim)
    tab = pl.BlockSpec((TM, LANES), lambda i: (jnp.where(i < npt, 0, 1 + (i - npt) % tps), 0))
    cache = lambda w: pl.BlockSpec((TM, w), lambda i: (jnp.minimum(i, npt - 1), 0))
    return pl.pallas_call(
        functools.partial(_qkv_kernel, npt, len(xs)),
        grid=(nt,),
        in_specs=_row_specs(xs, npt) + [
            pl.BlockSpec((1, 1, N_MOD * D_MODEL), lambda i: (layer * 3 + _cond_row(i, npt, tps), 0, 0)),
            full(g_mix), full(w_in_bf), full(gq), full(gk), tab, tab, full(seg),
        ],
        out_specs=[row(A_W), row(A_W), row(A_W), row(BQ_W), row(2 * BKV_W), row(BKV_W),
                   cache(A_W), cache(A_W), cache(BKV_W), cache(BKV_W)],
        out_shape=[
            jax.ShapeDtypeStruct((t, A_W), BF16), jax.ShapeDtypeStruct((t, A_W), BF16),
            jax.ShapeDtypeStruct((t, A_W), BF16), jax.ShapeDtypeStruct((t, BQ_W), BF16),
            jax.ShapeDtypeStruct((t, 2 * BKV_W), BF16), jax.ShapeDtypeStruct((t, BKV_W), BF16),
            jax.ShapeDtypeStruct((tp, A_W), F32), jax.ShapeDtypeStruct((tp, A_W), F32),
            jax.ShapeDtypeStruct((tp, BKV_W), F32), jax.ShapeDtypeStruct((tp, BKV_W), F32),
        ],
        compiler_params=_cparams(1),
        name="attn_qkv",
    )(*xs, mod3, g_mix, w_in_bf, gq, gk, cos_t, sin_t, seg)


def _softmax_pv(qq, ks, vs):
    ss = [_dot_nt(qq, k) for k in ks]
    m = ss[0].max(axis=-1, keepdims=True)
    for s in ss[1:]:
        m = jnp.maximum(m, s.max(axis=-1, keepdims=True))
    acc = None
    l = None
    for s, v in zip(ss, vs):
        p = jnp.exp(s - m)
        ls = p.sum(axis=-1, keepdims=True)
        pv = _dot(p.astype(BF16), v)
        acc = pv if acc is None else acc + pv
        l = ls if l is None else l + ls
    return acc / l


def _split_halves(q):
    lo = _lane_iota(q.shape) < (LANES // 2)
    zero = jnp.zeros_like(q)
    return jnp.concatenate([jnp.where(lo, q, zero), jnp.where(lo, zero, q)], axis=0)


def _attn_body(lambda_init, qa, qb, ka_segs, va_segs, kbd_segs, vb_segs, lam, subln):
    tq = qa.shape[0]
    cols = []
    for h in range(A_HEADS):
        sl = slice(h * LANES, (h + 1) * LANES)
        o = _softmax_pv(_split_halves(qa[:, sl]), [k[:, sl] for k in ka_segs],
                        [v[:, sl] for v in va_segs])
        od = o[:tq] - lam * o[tq:]
        cols.append(_rms(od, subln) * (1.0 - lambda_init))
    lo = _lane_iota((tq, LANES)) < B_HEAD_DIM
    for c in range(BQ_W // LANES):
        g = (2 * c) // (B_Q_HEADS // B_KV_HEADS)
        sl = slice(c * LANES, (c + 1) * LANES)
        gl = slice(g * LANES, (g + 1) * LANES)
        o = _softmax_pv(_split_halves(qb[:, sl]), [k[:, gl] for k in kbd_segs], vb_segs)
        oe, oo = o[:tq], o[tq:]
        if g == 0:
            cols.append(jnp.where(lo, oe, pltpu.roll(oo, B_HEAD_DIM, 1)))
        else:
            cols.append(jnp.where(lo, pltpu.roll(oe, B_HEAD_DIM, 1), oo))
    return jnp.concatenate(cols, axis=1).astype(BF16)


def _lambda(lq1, lk1, lq2, lk2, lambda_init):
    return (jnp.exp(jnp.sum(lq1 * lk1, axis=-1, keepdims=True))
            - jnp.exp(jnp.sum(lq2 * lk2, axis=-1, keepdims=True)) + lambda_init)


def _attn_prompt_kernel(lambda_init, qa, ka, va, qb, kbd, vb, lq1, lk1, lq2, lk2, subln, o_ref):
    lam = _lambda(lq1[...], lk1[...], lq2[...], lk2[...], lambda_init)
    o_ref[...] = _attn_body(lambda_init, qa[...], qb[...], [ka[...]], [va[...]],
                            [kbd[...]], [vb[...]], lam, subln[...])


def _attn_sample_kernel(lambda_init, qa, ka, va, qb, kbd, vb, cka, cva, ckbd, cvb,
                        lq1, lk1, lq2, lk2, subln, o_ref):
    lam = _lambda(lq1[...], lk1[...], lq2[...], lk2[...], lambda_init)
    o_ref[...] = _attn_body(lambda_init, qa[...], qb[...], [ka[...], cka[...]], [va[...], cva[...]],
                            [kbd[...], ckbd[...]], [vb[...], cvb[...]], lam, subln[...])


def _attention(lambda_init, qa, ka, va, qb, kbd, vb, cka, cva, ckbd, cvb,
               lq1, lk1, lq2, lk2, subln, n_prompt, prompt_len, n_sample, sample_len):
    npt = n_prompt * prompt_len // TM
    small = [lq1, lk1, lq2, lk2, subln]
    widths = [A_W, A_W, A_W, BQ_W, 2 * BKV_W, BKV_W]
    assert prompt_len == TM
    full1 = lambda a: pl.BlockSpec(a.shape, lambda b: (0,) * a.ndim)
    o_p = pl.pallas_call(
        functools.partial(_attn_prompt_kernel, lambda_init),
        grid=(n_prompt,),
        in_specs=[pl.BlockSpec((TM, w), lambda b: (b, 0)) for w in widths] + [full1(a) for a in small],
        out_specs=pl.BlockSpec((TM, D_MODEL), lambda b: (b, 0)),
        out_shape=jax.ShapeDtypeStruct((npt * TM, D_MODEL), BF16),
        compiler_params=_cparams(1),
        name="attn_prompt",
    )(qa, ka, va, qb, kbd, vb, *small)
    tps = sample_len // TM
    seq0 = n_prompt * prompt_len // sample_len
    assert seq0 * sample_len == n_prompt * prompt_len
    past = cka.shape[1]
    qspec = lambda w: pl.BlockSpec((TM, w), lambda b, q: (npt + b * tps + q, 0))
    kspec = lambda w: pl.BlockSpec((sample_len, w), lambda b, q: (seq0 + b, 0))
    cspec = lambda w: pl.BlockSpec((None, past, w), lambda b, q: (b, 0, 0))
    full2 = lambda a: pl.BlockSpec(a.shape, lambda b, q: (0,) * a.ndim)
    o_s = pl.pallas_call(
        functools.partial(_attn_sample_kernel, lambda_init),
        grid=(n_sample, tps),
        in_specs=[qspec(A_W), kspec(A_W), kspec(A_W), qspec(BQ_W), kspec(2 * BKV_W), kspec(BKV_W),
                  cspec(A_W), cspec(A_W), cspec(2 * BKV_W), cspec(BKV_W)] + [full2(a) for a in small],
        out_specs=pl.BlockSpec((TM, D_MODEL), lambda b, q: (b * tps + q, 0)),
        out_shape=jax.ShapeDtypeStruct((n_sample * sample_len, D_MODEL), BF16),
        compiler_params=_cparams(2),
        name="attn_sample",
    )(qa, ka, va, qb, kbd, vb, cka, cva, ckbd, cvb, *small)
    return o_p, o_s


def _router_tail(i, x, y, mod, g_ffn, wr, br, xo_ref, h_ref, ri_ref, rg_ref, cnt_ref, carry_ref):
    gate_mix = mod[:, 2 * D_MODEL:3 * D_MODEL]
    shift = mod[:, 3 * D_MODEL:4 * D_MODEL]
    scale = mod[:, 4 * D_MODEL:5 * D_MODEL]
    xn = x + gate_mix * y
    xo_ref[...] = xn
    h = _rms(xn, g_ffn) * (1.0 + scale) + shift
    _store_token_tiles(h_ref, h)
    h_hi = h.astype(BF16)
    h_lo = (h - h_hi.astype(F32)).astype(BF16)
    w_hi = wr.astype(BF16)
    w_lo = (wr - w_hi.astype(F32)).astype(BF16)
    logits = _dot(h_hi, w_hi) + _dot(h_lo, w_hi) + _dot(h_hi, w_lo) + br
    lane = _lane_iota(logits.shape)
    lane_f = lane.astype(F32)
    vals, idxs = [], []
    l = logits
    for _ in range(TOP_K):
        m = l.max(axis=-1, keepdims=True)
        idx = jnp.where(l == m, lane_f, float(LANES)).min(axis=-1, keepdims=True)
        vals.append(m)
        idxs.append(idx)
        l = jnp.where(lane_f == idx, -jnp.inf, l)
    es = [jnp.exp(v - vals[0]) for v in vals]
    den = es[0]
    for e in es[1:]:
        den = den + e
    sel = jnp.zeros(logits.shape, F32)
    for idx in idxs:
        sel = sel + jnp.where(lane_f == idx, 1.0, 0.0)

    @pl.when(i == 0)
    def _():
        carry_ref[...] = jnp.zeros_like(carry_ref)

    carry = carry_ref[...]
    r_io = lax.broadcasted_iota(I32, (TM, TM), 0)
    c_io = lax.broadcasted_iota(I32, (TM, TM), 1)
    tri = jnp.where(c_io < r_io, 1.0, 0.0).astype(BF16)
    rank = _dot(tri, sel.astype(BF16)) + carry
    carry = carry + sel.sum(axis=0, keepdims=True)
    carry_ref[...] = carry
    cnt_ref[...] = carry
    ri = jnp.zeros(logits.shape, F32)
    rg = jnp.zeros(logits.shape, F32)
    for k in range(TOP_K):
        rk = jnp.where(lane_f == idxs[k], rank, 0.0).sum(axis=-1, keepdims=True)
        ri = jnp.where(lane == k, idxs[k], ri)
        ri = jnp.where(lane == TOP_K + k, rk, ri)
        rg = jnp.where(lane == k, es[k] / den, rg)
    ri_ref[...] = ri.astype(I32)
    rg_ref[...] = rg


def _tail_specs(t):
    row = lambda w: pl.BlockSpec((TM, w), lambda i: (i, 0))
    out_specs = [row(D_MODEL), pl.BlockSpec((TM * ROW_TILES, LANES), lambda i: (i, 0)), row(LANES), row(LANES),
                 pl.BlockSpec((1, LANES), lambda i: (0, 0))]
    out_shape = [jax.ShapeDtypeStruct((t, D_MODEL), F32), jax.ShapeDtypeStruct((t * ROW_TILES, LANES), F32),
                 jax.ShapeDtypeStruct((t, LANES), I32), jax.ShapeDtypeStruct((t, LANES), F32),
                 jax.ShapeDtypeStruct((1, LANES), F32)]
    return out_specs, out_shape


def _post_attn_kernel(npt, n_x, *refs):
    x_refs = refs[:n_x]
    (op_ref, os_ref, mod_ref, w_ref, g_ref, wr_ref, br_ref,
     xo_ref, h_ref, ri_ref, rg_ref, cnt_ref, carry_ref) = refs[n_x:]
    i = pl.program_id(0)
    y = _dot(jnp.where(i < npt, op_ref[...], os_ref[...]), w_ref[...])
    _router_tail(i, _read_rows(i, npt, x_refs), y, mod_ref[0], g_ref[...], wr_ref[...], br_ref[...],
                 xo_ref, h_ref, ri_ref, rg_ref, cnt_ref, carry_ref)


def _post_attn_call(xs, o_p, o_s, mod3, layer, w_out_bf, g_ffn, wr, br, npt, tps):
    t = sum(x.shape[0] for x in xs)
    full = lambda a: pl.BlockSpec(a.shape, lambda i: (0,) * a.ndim)
    out_specs, out_shape = _tail_specs(t)
    return pl.pallas_call(
        functools.partial(_post_attn_kernel, npt, len(xs)),
        grid=(t // TM,),
        in_specs=_row_specs(xs, npt) + [
                  pl.BlockSpec((TM, D_MODEL), lambda i: (jnp.minimum(i, npt - 1), 0)),
                  pl.BlockSpec((TM, D_MODEL), lambda i: (jnp.maximum(i - npt, 0), 0)),
                  pl.BlockSpec((1, 1, N_MOD * D_MODEL), lambda i: (layer * 3 + _cond_row(i, npt, tps), 0, 0)),
                  full(w_out_bf), full(g_ffn), full(wr), full(br)],
        out_specs=out_specs, out_shape=out_shape,
        scratch_shapes=[pltpu.VMEM((1, LANES), F32)],
        compiler_params=_cparams(1),
        name="attn_out_router",
    )(*xs, o_p, o_s, mod3, w_out_bf, g_ffn, wr, br)


def _pool_kernel(npt, tps, x_ref, xp_ref, xn_ref, mod_ref, gm_ref, wp_ref, ps_ref, g_ref, wr_ref, br_ref,
                 xo_ref, h_ref, ri_ref, rg_ref, cnt_ref, carry_ref):
    i = pl.program_id(0)
    mod = mod_ref[0]
    shift = mod[:, 0:D_MODEL]
    scale = mod[:, D_MODEL:2 * D_MODEL]
    gm = gm_ref[...]
    x = x_ref[...]
    pre = lambda v: _rms(v, gm) * (1.0 + scale) + shift
    j = jnp.where(i < npt, 0, (i - npt) % tps)
    ntile = jnp.where(i < npt, 1, tps)
    has_prev = (j > 0).astype(F32)
    has_next = (j < ntile - 1).astype(F32)
    h = pre(x)
    hc = jnp.concatenate([pre(xp_ref[...]) * has_prev, h, pre(xn_ref[...]) * has_next], axis=0)
    rows = hc.shape[0]
    t_seq = (j * TM + lax.broadcasted_iota(I32, (TM, 1), 0)).astype(F32)
    seq_len = (ntile * TM).astype(F32)
    ys = []
    for g, w in enumerate(POOL_WINDOWS):
        half = w // 2
        s = hc[:, g * POOL_CH:(g + 1) * POOL_CH]
        step = 1
        while step < w:
            s = s + pltpu.roll(s, rows - step, 0)
            step *= 2
        s = pltpu.roll(s, half, 0) if half != HALO else s
        win = s[HALO:HALO + TM] if half != HALO else s[0:TM]
        cnt = jnp.minimum(t_seq + half, seq_len) - jnp.maximum(t_seq - half, 0.0)
        pooled = win / cnt - h[:, g * POOL_CH:(g + 1) * POOL_CH]
        ys.append(_dot(pooled.astype(BF16), wp_ref[g]))
    y = jnp.concatenate(ys, axis=1) * ps_ref[...]
    _router_tail(i, x, y, mod, g_ref[...], wr_ref[...], br_ref[...],
                 xo_ref, h_ref, ri_ref, rg_ref, cnt_ref, carry_ref)


def _pool_call(x, mod3, layer, g_mix, w_pool_bf, pool_scale, g_ffn, wr, br, npt, tps):
    t = x.shape[0]
    per = TM // HALO
    nh = t // HALO
    row = lambda w: pl.BlockSpec((TM, w), lambda i: (i, 0))
    full = lambda a: pl.BlockSpec(a.shape, lambda i: (0,) * a.ndim)
    out_specs, out_shape = _tail_specs(t)
    return pl.pallas_call(
        functools.partial(_pool_kernel, npt, tps),
        grid=(t // TM,),
        in_specs=[row(D_MODEL),
                  pl.BlockSpec((HALO, D_MODEL), lambda i: (jnp.maximum(i * per - 1, 0), 0)),
                  pl.BlockSpec((HALO, D_MODEL), lambda i: (jnp.minimum((i + 1) * per, nh - 1), 0)),
                  pl.BlockSpec((1, 1, N_MOD * D_MODEL), lambda i: (layer * 3 + _cond_row(i, npt, tps), 0, 0)),
                  full(g_mix), full(w_pool_bf), full(pool_scale), full(g_ffn), full(wr), full(br)],
        out_specs=out_specs, out_shape=out_shape,
        scratch_shapes=[pltpu.VMEM((1, LANES), F32)],
        compiler_params=_cparams(1),
        name="pool_router",
    )(x, x, x, mod3, g_mix, w_pool_bf, pool_scale, g_ffn, wr, br)


def _expert_rows(x, e, wgu_bf, wd_bf, bgu_ref, bd_ref):
    gu = _dot(x, wgu_bf[...]) + bgu_ref[pl.ds(e, 1), :]
    gate = jnp.minimum(gu[:, :D_EXPERT], SWIGLU_LIMIT)
    up = jnp.clip(gu[:, D_EXPERT:], -SWIGLU_LIMIT, SWIGLU_LIMIT)
    act = gate * jax.nn.sigmoid(SWIGLU_ALPHA * gate) * (up + 1.0)
    return _dot(act.astype(BF16), wd_bf[...]) + bd_ref[pl.ds(e, 1), :]


ROW_GROUP = 16
COMBINE_TILES = 2


def _for_rows(n, fn):
    def group(g, c):
        for q in range(ROW_GROUP):
            fn(g * ROW_GROUP + q, q % 2)
        return c

    def single(r, c):
        fn(r, 0)
        return c
    full = n // ROW_GROUP
    lax.fori_loop(0, full, group, 0)
    lax.fori_loop(full * ROW_GROUP, n, single, 0)


def _gmm_kernel(layer, n_tok, pos_ref, te_ref, nv_ref, nxt_ref,
                h_hbm, wgu_hbm, wd_hbm, bgu_ref, bd_ref, yk_hbm,
                src_ref, cur_ref, xbuf, ybuf, wgu_st, wd_st, wgu_bf, wd_bf, sem_g, sem_s, sem_w):
    w = pl.program_id(0)
    n_steps = pl.num_programs(0)
    slot = w % 2
    nv = nv_ref[w]
    nv_next = jnp.where(w + 1 < n_steps, nv_ref[jnp.minimum(w + 1, n_steps - 1)], 0)

    def weights_copy(e):
        return (pltpu.make_async_copy(wgu_hbm.at[layer, e], wgu_st, sem_w.at[0]),
                pltpu.make_async_copy(wd_hbm.at[layer, e], wd_st, sem_w.at[1]))

    def tile_of(ref, start):
        return ref.at[pl.ds(pl.multiple_of(start, ROW_TILES), ROW_TILES), :]

    def rows_of(ref, n):
        return ref.at[pl.ds(0, pl.multiple_of(n * ROW_TILES, ROW_TILES)), :]

    def gather_start(tile, dst_slot, n):
        def one(r, prio):
            tok_row = src_ref[tile * TM + r] & (n_tok * ROW_TILES - 1)
            pltpu.make_async_copy(tile_of(h_hbm, tok_row), tile_of(xbuf.at[dst_slot], r * ROW_TILES),
                                  sem_g.at[dst_slot]).start(priority=prio)
        _for_rows(n, one)

    def gather_wait(dst_slot, n):
        pltpu.make_async_copy(rows_of(h_hbm, n), rows_of(xbuf.at[dst_slot], n), sem_g.at[dst_slot]).wait()

    def scatter_start(tile, src_slot, n):
        def one(r, prio):
            pltpu.make_async_copy(tile_of(ybuf.at[src_slot], r * ROW_TILES),
                                  tile_of(yk_hbm, src_ref[tile * TM + r]),
                                  sem_s.at[src_slot]).start(priority=prio)
        _for_rows(n, one)

    def scatter_wait(src_slot, n):
        pltpu.make_async_copy(rows_of(ybuf.at[src_slot], n), rows_of(yk_hbm, n), sem_s.at[src_slot]).wait()

    @pl.when(w == 0)
    def _():
        def invert(a, c):
            src_ref[pos_ref[a]] = a * ROW_TILES
            return c
        lax.fori_loop(0, n_tok * TOP_K, invert, 0, unroll=8)
        xbuf[...] = jnp.zeros_like(xbuf)
        cur_ref[0] = -1
        for cp in weights_copy(te_ref[0]):
            cp.start()
        gather_start(0, 0, nv)

    @pl.when(nv > 0)
    def _():
        e = te_ref[w]
        gather_wait(slot, nv)

        @pl.when(nv_next > 0)
        def _():
            gather_start(w + 1, 1 - slot, nv_next)

        @pl.when(w >= 2)
        def _():
            scatter_wait(slot, nv_ref[jnp.maximum(w - 2, 0)])

        @pl.when(cur_ref[0] != e)
        def _():
            for cp in weights_copy(e):
                cp.wait()
            wgu_bf[...] = wgu_st[...].astype(BF16)
            wd_bf[...] = wd_st[...].astype(BF16)
            cur_ref[0] = e

            @pl.when(nxt_ref[e] < N_EXPERTS)
            def _():
                for cp in weights_copy(nxt_ref[e]):
                    cp.start()

        def run(rows):
            x = _load_token_tiles(xbuf.at[slot], rows).astype(BF16)
            _store_token_tiles(ybuf.at[slot], _expert_rows(x, e, wgu_bf, wd_bf, bgu_ref, bd_ref))

        @pl.when(nv > TM // 2)
        def _():
            run(TM)

        @pl.when(nv <= TM // 2)
        def _():
            run(TM // 2)

        scatter_start(w, slot, nv)

        @pl.when(nv_next == 0)
        def _():
            @pl.when(w >= 1)
            def _():
                scatter_wait(1 - slot, nv_ref[jnp.maximum(w - 1, 0)])
            scatter_wait(slot, nv)


def _gmm_call(pos, te, nv, nxt, h, w_gu, w_dn, b_gu, b_dn, layer):
    assert ROW_TILES == SUBLANES
    t = h.shape[0] // ROW_TILES
    assert t & (t - 1) == 0
    n_rows = t * TOP_K
    n_steps = te.shape[0]
    return pl.pallas_call(
        functools.partial(_gmm_kernel, layer, t),
        grid_spec=pltpu.PrefetchScalarGridSpec(
            num_scalar_prefetch=4, grid=(n_steps,),
            in_specs=[pl.BlockSpec(memory_space=pl.ANY), pl.BlockSpec(memory_space=pl.ANY),
                      pl.BlockSpec(memory_space=pl.ANY),
                      pl.BlockSpec((None, N_EXPERTS, 2 * D_EXPERT), lambda w, *_: (layer, 0, 0)),
                      pl.BlockSpec((None, N_EXPERTS, D_MODEL), lambda w, *_: (layer, 0, 0))],
            out_specs=pl.BlockSpec(memory_space=pl.ANY),
            scratch_shapes=[
                pltpu.SMEM((n_steps * TM,), I32), pltpu.SMEM((1,), I32),
                pltpu.VMEM((2, TM * ROW_TILES, LANES), F32), pltpu.VMEM((2, TM * ROW_TILES, LANES), F32),
                pltpu.VMEM((D_MODEL, 2 * D_EXPERT), F32), pltpu.VMEM((D_EXPERT, D_MODEL), F32),
                pltpu.VMEM((D_MODEL, 2 * D_EXPERT), BF16), pltpu.VMEM((D_EXPERT, D_MODEL), BF16),
                pltpu.SemaphoreType.DMA((2,)), pltpu.SemaphoreType.DMA((2,)), pltpu.SemaphoreType.DMA((2,))]),
        out_shape=jax.ShapeDtypeStruct((n_rows * ROW_TILES, LANES), F32),
        compiler_params=_cparams(1),
        name="moe_experts",
    )(pos, te, nv, nxt, h, w_gu, w_dn, b_gu, b_dn)


def _combine_kernel(final, npt, x_ref, rg_ref, mod_ref, gf_ref, y0_ref, y1_ref, y2_ref, y3_ref, *outs):
    i = pl.program_id(0)
    rows = x_ref.shape[0]
    rg = rg_ref[...]
    moe = rg[:, 0:1] * _load_token_tiles(y0_ref, rows)
    for k, y_ref in enumerate((y1_ref, y2_ref, y3_ref), start=1):
        moe = moe + rg[:, k:k + 1] * _load_token_tiles(y_ref, rows)
    gate_ffn = mod_ref[0][:, 5 * D_MODEL:6 * D_MODEL]
    xn = x_ref[...] + gate_ffn * moe
    if not final:
        outs[0][...] = xn
        return
    y = _rms(xn, gf_ref[...])
    yp_ref, ys_ref = outs

    @pl.when(i < npt)
    def _():
        yp_ref[...] = y

    @pl.when(i >= npt)
    def _():
        ys_ref[...] = y


def _combine_call(yk, x, rg, mod3, layer, g_final, final, npt, tps):
    t = x.shape[0]
    assert TOP_K == 4 and npt % COMBINE_TILES == 0 and tps % COMBINE_TILES == 0
    tm = TM * COMBINE_TILES
    npt, tps, nt = npt // COMBINE_TILES, tps // COMBINE_TILES, t // tm
    row = lambda w: pl.BlockSpec((tm, w), lambda i: (i, 0))
    slab = lambda k: pl.BlockSpec((tm * ROW_TILES, LANES), lambda i: (k * nt + i, 0))
    if final:
        out_specs = [pl.BlockSpec((tm, D_MODEL), lambda i: (jnp.minimum(i, npt - 1), 0)),
                     pl.BlockSpec((tm, D_MODEL), lambda i: (jnp.maximum(i - npt, 0), 0))]
        out_shape = [jax.ShapeDtypeStruct((npt * tm, D_MODEL), F32),
                     jax.ShapeDtypeStruct(((nt - npt) * tm, D_MODEL), F32)]
    else:
        out_specs = [row(D_MODEL)]
        out_shape = [jax.ShapeDtypeStruct((t, D_MODEL), F32)]
    return pl.pallas_call(
        functools.partial(_combine_kernel, final, npt),
        grid=(nt,),
        in_specs=[row(D_MODEL), row(LANES),
                  pl.BlockSpec((1, 1, N_MOD * D_MODEL), lambda i: (layer * 3 + _cond_row(i, npt, tps), 0, 0)),
                  pl.BlockSpec(g_final.shape, lambda i: (0, 0))] + [slab(k) for k in range(TOP_K)],
        out_specs=out_specs, out_shape=out_shape,
        compiler_params=_cparams(1),
        name="moe_combine",
    )(x, rg, mod3, g_final, yk, yk, yk, yk)


def _routing_tables(ri, cnt, n_tok):
    ex = jnp.arange(N_EXPERTS, dtype=I32)
    counts = cnt[0, :N_EXPERTS].astype(I32)
    tiles = (counts + (TM - 1)) // TM
    tile_end = jnp.sum(jnp.where(ex[None, :] <= ex[:, None], tiles[None, :], 0), axis=1)
    tile_start = tile_end - tiles
    e_idx = ri[:, :TOP_K]
    rank = ri[:, TOP_K:2 * TOP_K]
    onehot = e_idx[:, :, None] == ex[None, None, :]
    pos = rank + jnp.sum(jnp.where(onehot, tile_start[None, None, :] * TM, 0), axis=-1)
    pos = pos.T.reshape(-1).astype(I32)
    n_steps = n_tok * TOP_K // TM + N_EXPERTS
    w = jnp.arange(n_steps, dtype=I32)
    te = jnp.minimum(jnp.sum((tile_end[None, :] <= w[:, None]).astype(I32), axis=1), N_EXPERTS - 1)
    mine = te[:, None] == ex[None, :]
    left = jnp.sum(jnp.where(mine, counts[None, :] - (w[:, None] - tile_start[None, :]) * TM, 0), axis=1)
    nv = jnp.where(w < tile_end[-1], jnp.clip(left, 0, TM), 0)
    later = (ex[None, :] > ex[:, None]) & (counts[None, :] > 0)
    nxt = jnp.min(jnp.where(later, ex[None, :], N_EXPERTS), axis=1)
    return pos, te.astype(I32), nv.astype(I32), nxt.astype(I32)


def _moe(x, h, ri, rg, cnt, mod3, layer, w_gu, b_gu, w_dn, b_dn, g_final, final, npt, tps):
    pos, te, nv, nxt = _routing_tables(ri, cnt, x.shape[0])
    yk = _gmm_call(pos, te, nv, nxt, h, w_gu, w_dn, b_gu, b_dn, layer)
    return _combine_call(yk, x, rg, mod3, layer, g_final, final, npt, tps)


def _rope_tables(n_tokens):
    n_rows = n_tokens // GRID_W
    rows = jnp.repeat(jnp.arange(n_rows, dtype=F32), GRID_W)
    cols = jnp.tile(jnp.arange(GRID_W, dtype=F32), n_rows)
    axis_dim = A_HEAD_DIM // 2
    inv = ROPE_THETA ** (-jnp.arange(0, axis_dim, 2, dtype=F32) / axis_dim)
    ang_r = rows[:, None] * inv[None, :]
    ang_c = cols[:, None] * inv[None, :]
    ang = jnp.concatenate([ang_r, ang_r, ang_c, ang_c], axis=-1)
    cos, sin = jnp.cos(ang), jnp.sin(ang)
    sign = jnp.where((jnp.arange(A_HEAD_DIM) % 32) < 16, -1.0, 1.0).astype(F32)
    sin = sin * sign[None, :]
    cos = jnp.concatenate([jnp.ones((TM, A_HEAD_DIM), F32), cos], axis=0)
    sin = jnp.concatenate([jnp.zeros((TM, A_HEAD_DIM), F32), sin], axis=0)
    return jnp.tile(cos, (1, LANES // A_HEAD_DIM)), jnp.tile(sin, (1, LANES // A_HEAD_DIM))


def kernel(x_prompt, x_sample, cache_diff_k, cache_diff_v, cache_gqa_k, cache_gqa_v, c, c_ctx, w_ada, b_ada, norm_mix, norm_ffn, norm_final, w_attn_in, w_attn_out, lam_q1, lam_k1, lam_q2, lam_k2, diff_subln, gqa_q_norm, gqa_k_norm, w_pool, pool_scale, w_router, b_router, w_gate_up, b_gate_up, w_down, b_down):
    n_prompt, prompt_len, d = x_prompt.shape
    n_sample, sample_len, _ = x_sample.shape
    assert d == D_MODEL and prompt_len % TM == 0 and sample_len % TM == 0
    assert n_sample + 1 <= SUBLANES
    npt = n_prompt * prompt_len // TM
    tps = sample_len // TM
    tp = n_prompt * prompt_len
    past = cache_diff_k.shape[2]

    xs = (x_prompt.reshape(tp, d), x_sample.reshape(n_sample * sample_len, d))
    cond8 = jnp.zeros((SUBLANES, d), F32).at[0].set(c_ctx).at[1:1 + n_sample].set(c)
    mod = _modulation_all(cond8, w_ada, b_ada)
    mod3 = mod[:, :1 + n_sample].reshape(DEPTH * (1 + n_sample), 1, N_MOD * d)
    assert n_sample == 2

    cos_t, sin_t = _rope_tables(sample_len)
    seg_r = jnp.arange(BQ_W)[:, None] // B_HEAD_DIM
    seg = (seg_r == seg_r.T).astype(BF16)

    caches = []
    y_final = None
    for i in range(DEPTH):
        j = i // 2
        g_mix = norm_mix[i][None]
        g_ffn = norm_ffn[i][None]
        wr = jnp.zeros((d, LANES), F32).at[:, :N_EXPERTS].set(w_router[i])
        br = jnp.full((1, LANES), NEG_BIG, F32).at[0, :N_EXPERTS].set(b_router[i])
        if i % 2 == 0:
            lambda_init = 0.8 - 0.6 * math.exp(-0.3 * i)
            gq = jnp.tile(gqa_q_norm[j], BQ_W // B_HEAD_DIM)[None]
            gk = jnp.tile(gqa_k_norm[j], BKV_W // B_HEAD_DIM)[None]
            qa, ka, va, qb, kbd, vb, ck, cv, cgk, cgv = _qkv_call(
                xs, mod3, i, g_mix, w_attn_in[j].astype(BF16), gq, gk, cos_t, sin_t, seg, npt, tps)
            caches.append((ck, cv, cgk, cgv))
            cka = cache_diff_k[:, j].reshape(n_sample, past, A_W).astype(BF16)
            cva = cache_diff_v[:, j].reshape(n_sample, past, A_W).astype(BF16)
            gk_c = cache_gqa_k[:, j]
            ckbd = jnp.concatenate([gk_c[:, :, 0], gk_c[:, :, 0], gk_c[:, :, 1], gk_c[:, :, 1]],
                                   axis=-1).astype(BF16)
            cvb = cache_gqa_v[:, j].reshape(n_sample, past, BKV_W).astype(BF16)
            o_p, o_s = _attention(lambda_init, qa, ka, va, qb, kbd, vb, cka, cva, ckbd, cvb,
                                  lam_q1[j][None], lam_k1[j][None], lam_q2[j][None], lam_k2[j][None],
                                  diff_subln[j][None], n_prompt, prompt_len, n_sample, sample_len)
            x, h, ri, rg, cnt = _post_attn_call(xs, o_p, o_s, mod3, i, w_attn_out[j].astype(BF16), g_ffn, wr, br,
                                                npt, tps)
        else:
            x, h, ri, rg, cnt = _pool_call(x, mod3, i, g_mix, w_pool[j].astype(BF16), pool_scale[j][None],
                                           g_ffn, wr, br, npt, tps)
        final = i == DEPTH - 1
        outs = _moe(x, h, ri, rg, cnt, mod3, i, w_gate_up, b_gate_up, w_down, b_down, norm_final[None], final, npt, tps)
        if final:
            y_final = outs
        else:
            x = outs[0]
            xs = (x,)

    y_prompt = y_final[0].reshape(n_prompt, prompt_len, d)
    y_sample = y_final[1].reshape(n_sample, sample_len, d)
    stack = lambda k, shp: jnp.stack([cc[k].reshape(shp) for cc in caches], axis=1)
    new_diff_k = stack(0, (n_prompt, prompt_len, A_HEADS, 2 * A_HEAD_DIM))
    new_diff_v = stack(1, (n_prompt, prompt_len, A_HEADS, 2 * A_HEAD_DIM))
    new_gqa_k = stack(2, (n_prompt, prompt_len, B_KV_HEADS, B_HEAD_DIM))
    new_gqa_v = stack(3, (n_prompt, prompt_len, B_KV_HEADS, B_HEAD_DIM))
    return (y_prompt, y_sample, new_diff_k, new_diff_v, new_gqa_k, new_gqa_v)
```
